```python
import jax, jax.numpy as jnp
from jax import lax
import numpy as np

D_MODEL = 1024
BATCH = 4
SEQ = 4096
DEPTH = 2

CHUNK = 64
D_SC = D_MODEL
SC_WIDTH = 3
D_CF = D_MODEL
CF_WIDTH = 31
D_POOL = D_MODEL
POOL_WINDOWS = (2, 4, 8, 16)
N_POOL_GROUPS = len(POOL_WINDOWS)
POOL_GROUP = D_POOL // N_POOL_GROUPS
N_BRANCHES = 3
SPLITS = np.cumsum([D_SC, D_SC, D_SC, D_CF, D_CF, D_POOL]).tolist()
IN_COLS = 3 * D_SC + 2 * D_CF + D_POOL + N_BRANCHES * D_MODEL
N_EXPERTS = 32
TOP_K = 4
D_FF = D_MODEL
SWIGLU_LIMIT = 7.0
SWIGLU_ALPHA = 1.702
EXPERT_BLOCK = 128
NORM_EPS = 1e-6
LN_EPS = 1e-5
N_MOD = 6

kernel_name = "hybrid_conv_pool_moe_streaming_block"


def rms_norm(x, g):
    x32 = x.astype(jnp.float32)
    y = x32 * lax.rsqrt(jnp.mean(x32 * x32, axis=-1, keepdims=True) + NORM_EPS)
    return (y * g.astype(jnp.float32)).astype(x.dtype)


def layer_norm(x, g, b):
    x32 = x.astype(jnp.float32)
    mu = jnp.mean(x32, axis=-1, keepdims=True)
    var = jnp.mean(jnp.square(x32 - mu), axis=-1, keepdims=True)
    y = (x32 - mu) * lax.rsqrt(var + LN_EPS)
    return (y * g.astype(jnp.float32) + b.astype(jnp.float32)).astype(x.dtype)


def causal_depthwise_conv(x, w):
    k = w.shape[0]
    return lax.conv_general_dilated(
        x, w[:, None, :].astype(x.dtype), window_strides=(1,), padding=[(k - 1, 0)],
        dimension_numbers=("NWC", "WIO", "NWC"), feature_group_count=x.shape[-1])


def multiscale_pool(u):
    t = u.shape[1]
    u32 = u.astype(jnp.float32)
    cs = jnp.cumsum(u32, axis=1)
    pos = jnp.arange(1, t + 1, dtype=jnp.float32)[None, :, None]
    outs = []
    for g, w in enumerate(POOL_WINDOWS):
        cg = cs[..., g * POOL_GROUP:(g + 1) * POOL_GROUP]
        lag = jnp.pad(cg[:, :t - w], ((0, 0), (w, 0), (0, 0)))
        outs.append((cg - lag) / jnp.minimum(pos, w))
    return (jnp.concatenate(outs, axis=-1) - u32).astype(u.dtype)


def mixer(h, w_in, sc_conv, sc_out, cf_conv, cf_conv_b, cf_ln_g, cf_ln_b, cf_out, cf_out_b,
          pool_w, pool_scale, w_o):
    bsz, t, d = h.shape
    proj = h @ w_in
    b_sc, c_sc, x_sc, a_cf, g_cf, u_pool, gate_logits = jnp.split(proj, SPLITS, axis=-1)
    y_a = (b_sc * causal_depthwise_conv(c_sc * x_sc, sc_conv)) @ sc_out
    v = a_cf * jax.nn.sigmoid(g_cf)
    v = causal_depthwise_conv(v, cf_conv) + cf_conv_b
    v = jax.nn.silu(layer_norm(v, cf_ln_g, cf_ln_b))
    y_b = v @ cf_out + cf_out_b
    p = multiscale_pool(u_pool).reshape(bsz, t, N_POOL_GROUPS, POOL_GROUP)
    y_c = jnp.einsum("btgi,gio->btgo", p, pool_w).reshape(bsz, t, d) * pool_scale
    gates = jax.nn.sigmoid(gate_logits).reshape(bsz, t, N_BRANCHES, d)
    merged = gates[:, :, 0] * y_a + gates[:, :, 1] * y_b + gates[:, :, 2] * y_c
    return merged @ w_o


def moe(h, w_router, b_router, w_gu, b_gu, w_down, b_down):
    bsz, t, d = h.shape
    n_tok = bsz * t
    ht = h.reshape(n_tok, d)
    logits = (ht @ w_router + b_router).astype(jnp.float32)
    top_val, top_idx = lax.top_k(logits, TOP_K)
    gate = jax.nn.softmax(top_val, axis=-1).astype(h.dtype)
    n_asg = n_tok * TOP_K
    flat_e = top_idx.reshape(n_asg).astype(jnp.int32)
    flat_tok = jnp.arange(n_asg, dtype=jnp.int32) // TOP_K
    flat_g = gate.reshape(n_asg)
    se, order = lax.sort_key_val(flat_e, jnp.arange(n_asg, dtype=jnp.int32))
    st = flat_tok[order]
    sg = flat_g[order]
    counts = jnp.bincount(flat_e, length=N_EXPERTS).astype(jnp.int32)
    padded = (counts + EXPERT_BLOCK - 1) // EXPERT_BLOCK * EXPERT_BLOCK
    ustart = jnp.cumsum(counts) - counts
    pend = jnp.cumsum(padded)
    pstart = pend - padded
    dest = pstart[se] + jnp.arange(n_asg, dtype=jnp.int32) - ustart[se]
    n_pad = n_asg + N_EXPERTS * EXPERT_BLOCK
    n_blocks = n_pad // EXPERT_BLOCK
    pad_tok = jnp.zeros((n_pad,), jnp.int32).at[dest].set(st)
    block_e = jnp.minimum(
        jnp.searchsorted(pend, jnp.arange(n_blocks, dtype=jnp.int32) * EXPERT_BLOCK, side="right"),
        N_EXPERTS - 1).astype(jnp.int32)
    xb = ht[pad_tok].reshape(n_blocks, EXPERT_BLOCK, d)

    def run_block(args):
        xblk, e = args
        hg = xblk @ w_gu[e] + b_gu[e]
        gt = jnp.minimum(hg[:, :D_FF], SWIGLU_LIMIT)
        up = jnp.clip(hg[:, D_FF:], -SWIGLU_LIMIT, SWIGLU_LIMIT)
        act = (up + 1.0) * gt * jax.nn.sigmoid(gt * SWIGLU_ALPHA)
        return act @ w_down[e] + b_down[e]

    yb = lax.map(run_block, (xb, block_e)).reshape(n_pad, d)
    y = jax.ops.segment_sum(yb[dest] * sg[:, None], st, num_segments=n_tok)
    return y.reshape(bsz, t, d)


def setup_inputs(seed: int = 0) -> dict:
    key = jax.random.key(seed)
    ks = jax.random.split(key, 32)
    f32 = jnp.float32
    d = D_MODEL

    def nrm(k, shape, scale):
        return jax.random.normal(k, shape, f32) * scale

    def gain(k, shape):
        return 1.0 + 0.05 * jax.random.normal(k, shape, f32)

    return {
        "x": nrm(ks[0], (BATCH, SEQ, d), 1.0),
        "c": nrm(ks[1], (BATCH, d), 1.0),
        "ada_w": nrm(ks[2], (DEPTH, d, N_MOD * d), 0.5 * d ** -0.5),
        "ada_b": nrm(ks[3], (DEPTH, N_MOD * d), 0.01),
        "mix_pre_g": gain(ks[4], (DEPTH, d)),
        "mix_post_g": gain(ks[5], (DEPTH, d)),
        "w_in": nrm(ks[6], (DEPTH, d, IN_COLS), d ** -0.5),
        "sc_conv": nrm(ks[7], (DEPTH, SC_WIDTH, D_SC), SC_WIDTH ** -0.5),
        "sc_out": nrm(ks[8], (DEPTH, D_SC, d), D_SC ** -0.5),
        "cf_conv": nrm(ks[9], (DEPTH, CF_WIDTH, D_CF), CF_WIDTH ** -0.5),
        "cf_conv_b": nrm(ks[10], (DEPTH, D_CF), 0.01),
        "cf_ln_g": gain(ks[11], (DEPTH, D_CF)),
        "cf_ln_b": nrm(ks[12], (DEPTH, D_CF), 0.01),
        "cf_out": nrm(ks[13], (DEPTH, D_CF, d), D_CF ** -0.5),
        "cf_out_b": nrm(ks[14], (DEPTH, d), 0.01),
        "pool_w": nrm(ks[15], (DEPTH, N_POOL_GROUPS, POOL_GROUP, POOL_GROUP), POOL_GROUP ** -0.5),
        "pool_scale": gain(ks[16], (DEPTH, d)),
        "w_o": nrm(ks[17], (DEPTH, d, d), d ** -0.5),
        "moe_pre_g": gain(ks[18], (DEPTH, d)),
        "moe_post_g": gain(ks[19], (DEPTH, d)),
        "w_router": nrm(ks[20], (DEPTH, d, N_EXPERTS), d ** -0.5),
        "b_router": nrm(ks[21], (DEPTH, N_EXPERTS), 0.01),
        "w_gu": nrm(ks[22], (DEPTH, N_EXPERTS, d, 2 * D_FF), d ** -0.5),
        "b_gu": nrm(ks[23], (DEPTH, N_EXPERTS, 2 * D_FF), 0.01),
        "w_down": nrm(ks[24], (DEPTH, N_EXPERTS, D_FF, d), D_FF ** -0.5),
        "b_down": nrm(ks[25], (DEPTH, N_EXPERTS, d), 0.01),
    }


def reference(x, c, ada_w, ada_b, mix_pre_g, mix_post_g, w_in, sc_conv, sc_out, cf_conv, cf_conv_b,
              cf_ln_g, cf_ln_b, cf_out, cf_out_b, pool_w, pool_scale, w_o, moe_pre_g, moe_post_g,
              w_router, b_router, w_gu, b_gu, w_down, b_down):
    bsz, _, d = x.shape
    c_act = jax.nn.silu(c)
    for l in range(DEPTH):
        mod = (c_act @ ada_w[l] + ada_b[l]).reshape(bsz, N_MOD, 1, d)
        shift1, scale1, gate1, shift2, scale2, gate2 = (mod[:, i] for i in range(N_MOD))
        h = rms_norm(x, mix_pre_g[l]) * (1.0 + scale1) + shift1
        y = mixer(h, w_in[l], sc_conv[l], sc_out[l], cf_conv[l], cf_conv_b[l], cf_ln_g[l], cf_ln_b[l],
                  cf_out[l], cf_out_b[l], pool_w[l], pool_scale[l], w_o[l])
        x = x + gate1 * rms_norm(y, mix_post_g[l])
        h = rms_norm(x, moe_pre_g[l]) * (1.0 + scale2) + shift2
        y = moe(h, w_router[l], b_router[l], w_gu[l], b_gu[l], w_down[l], b_down[l])
        x = x + gate2 * rms_norm(y, moe_post_g[l])
    return x
```

```python
import functools

import jax
import jax.numpy as jnp
from jax import lax
from jax.experimental import pallas as pl
from jax.experimental.pallas import tpu as pltpu

F32 = jnp.float32
BF16 = jnp.bfloat16

N_EXPERTS = 32
TOP_K = 4
N_MOD = 6
SC_WIDTH = 3
CF_WIDTH = 31
POOL_WINDOWS = (2, 4, 8, 16)
SWIGLU_LIMIT = 7.0
SWIGLU_ALPHA = 1.702
NORM_EPS = 1e-6
LN_EPS = 1e-5

SUBLANES = 8
TM = 256
BM = 256
SC_HALO = 8
CF_HALO = 32
POOL_HALO = 16
ROW_CHUNK = 32
COL_CHUNK = 256
VMEM_LIMIT = 60 * 1024 * 1024


def _rms(x, g):
    return x * lax.rsqrt(jnp.mean(x * x, axis=-1, keepdims=True) + NORM_EPS) * g


def _sigmoid(x):
    return 1.0 / (1.0 + jnp.exp(-x))


def _const_spec(shape):
    nd = len(shape)
    return pl.BlockSpec(shape, lambda *_: (0,) * nd, pipeline_mode=pl.Buffered(1))


def _ada_kernel(c_ref, w_ref, b_ref, o_ref):
    c = c_ref[...]
    c_act = c * _sigmoid(c)
    o_ref[0, 0] = jnp.dot(c_act, w_ref[0], preferred_element_type=F32,
                          precision=lax.Precision.HIGHEST) + b_ref[0, 0]


def _ada_call(c, ada_w, ada_b):
    depth, d, _ = ada_w.shape
    bsz = c.shape[0]
    ada_b4 = ada_b.reshape(depth, N_MOD, 1, d)
    out = pl.pallas_call(
        _ada_kernel,
        grid=(depth, N_MOD),
        in_specs=[
            pl.BlockSpec((bsz, d), lambda l, j: (0, 0)),
            pl.BlockSpec((1, d, d), lambda l, j: (l, 0, j)),
            pl.BlockSpec((1, 1, 1, d), lambda l, j: (l, j, 0, 0)),
        ],
        out_specs=pl.BlockSpec((1, 1, bsz, d), lambda l, j: (l, j, 0, 0)),
        out_shape=jax.ShapeDtypeStruct((depth, N_MOD, bsz, d), F32),
        name="ada_mod",
    )(c, ada_w, ada_b4)
    return jnp.transpose(out, (0, 2, 1, 3))


def _causal_conv(buf_ref, w_ref, width, halo, out_ref, bias_ref=None):
    tm, d = out_ref.shape
    for c0 in range(0, d, COL_CHUNK):
        cols = slice(c0, c0 + COL_CHUNK)
        for r0 in range(0, tm, ROW_CHUNK):
            acc = None
            for k in range(width):
                start = halo + r0 - (width - 1) + k
                term = w_ref[k:k + 1, cols] * buf_ref[start:start + ROW_CHUNK, cols]
                acc = term if acc is None else acc + term
            if bias_ref is not None:
                acc = acc + bias_ref[:, cols]
            out_ref[r0:r0 + ROW_CHUNK, cols] = acc


def _mix_kernel(x_ref, mod_ref, pre_g, post_g, w_in, sc_conv, sc_out, cf_conv, cf_conv_b, cf_ln_g,
                cf_ln_b, cf_out, cf_out_b, pool_w, pool_scale, w_o, moe_pre_g, w_router_t, b_router,
                x1_ref, h2_ref, eidx_ref, gate_ref, rank_ref, counts_ref,
                cx_buf, v_buf, u_buf, tmp_buf, carry_ref, *, steps_per_seq):
    i = pl.program_id(0)
    t = i % steps_per_seq
    tm, d = x_ref.shape

    @pl.when(t == 0)
    def _():
        cx_buf[0:SC_HALO, :] = jnp.zeros((SC_HALO, d), F32)
        v_buf[0:CF_HALO, :] = jnp.zeros((CF_HALO, d), F32)
        u_buf[0:POOL_HALO, :] = jnp.zeros((POOL_HALO, d), F32)

    @pl.when(i == 0)
    def _():
        carry_ref[...] = jnp.zeros(carry_ref.shape, F32)

    x = x_ref[...]
    shift1, scale1, gate1 = mod_ref[0, 0:1, :], mod_ref[0, 1:2, :], mod_ref[0, 2:3, :]
    shift2, scale2 = mod_ref[0, 3:4, :], mod_ref[0, 4:5, :]

    hb = (_rms(x, pre_g[...]) * (1.0 + scale1) + shift1).astype(BF16)

    def proj(j):
        return jnp.dot(hb, w_in[:, j * d:(j + 1) * d], preferred_element_type=F32)

    cx_buf[SC_HALO:SC_HALO + tm, :] = proj(1) * proj(2)
    _causal_conv(cx_buf, sc_conv, SC_WIDTH, SC_HALO, tmp_buf)
    y_a = jnp.dot((proj(0) * tmp_buf[...]).astype(BF16), sc_out[...], preferred_element_type=F32)
    cx_buf[0:SC_HALO, :] = cx_buf[tm:tm + SC_HALO, :]
    merged = _sigmoid(proj(6)) * y_a

    v_buf[CF_HALO:CF_HALO + tm, :] = proj(3) * _sigmoid(proj(4))
    _causal_conv(v_buf, cf_conv, CF_WIDTH, CF_HALO, tmp_buf, cf_conv_b)
    v_buf[0:CF_HALO, :] = v_buf[tm:tm + CF_HALO, :]
    v = tmp_buf[...]
    mu = jnp.mean(v, axis=-1, keepdims=True)
    vc = v - mu
    var = jnp.mean(vc * vc, axis=-1, keepdims=True)
    vn = vc * lax.rsqrt(var + LN_EPS) * cf_ln_g[...] + cf_ln_b[...]
    vn = vn * _sigmoid(vn)
    y_b = jnp.dot(vn.astype(BF16), cf_out[...], preferred_element_type=F32) + cf_out_b[...]
    merged = merged + _sigmoid(proj(7)) * y_b

    u = proj(5)
    u_buf[POOL_HALO:POOL_HALO + tm, :] = u
    pos = (t * tm + 1 + lax.broadcasted_iota(jnp.int32, (tm, 1), 0)).astype(F32)
    group = d // len(POOL_WINDOWS)
    y_c_parts = []
    for g, w in enumerate(POOL_WINDOWS):
        cols = slice(g * group, (g + 1) * group)
        s = u[:, cols]
        for j in range(1, w):
            s = s + u_buf[POOL_HALO - j:POOL_HALO - j + tm, cols]
        p = s / jnp.minimum(pos, float(w)) - u[:, cols]
        y_c_parts.append(jnp.dot(p.astype(BF16), pool_w[g], preferred_element_type=F32))
    u_buf[0:POOL_HALO, :] = u_buf[tm:tm + POOL_HALO, :]
    y_c = jnp.concatenate(y_c_parts, axis=-1) * pool_scale[...]
    merged = merged + _sigmoid(proj(8)) * y_c

    y = jnp.dot(merged.astype(BF16), w_o[...], preferred_element_type=F32)
    x1 = x + gate1 * _rms(y, post_g[...])
    x1_ref[...] = x1

    h2 = _rms(x1, moe_pre_g[...]) * (1.0 + scale2) + shift2
    h2_ref[...] = h2
    logits = lax.dot_general(w_router_t[...], h2, (((1,), (1,)), ((), ())),
                             preferred_element_type=F32,
                             precision=lax.Precision.HIGHEST) + b_router[...]
    eio = lax.broadcasted_iota(jnp.int32, (N_EXPERTS, tm), 0)
    neg_inf = jnp.float32(-jnp.inf)
    vals, idxs = [], []
    for _ in range(TOP_K):
        m = jnp.max(logits, axis=0, keepdims=True)
        idx = jnp.min(jnp.where(logits == m, eio, N_EXPERTS), axis=0, keepdims=True)
        vals.append(m)
        idxs.append(idx)
        logits = jnp.where(eio == idx, neg_inf, logits)
    exps = [jnp.exp(v - vals[0]) for v in vals]
    denom = exps[0] + exps[1] + exps[2] + exps[3]
    onehot = jnp.zeros((N_EXPERTS, tm), F32)
    for idx in idxs:
        onehot = onehot + (eio == idx).astype(F32)
    tri = (lax.broadcasted_iota(jnp.int32, (tm, tm), 0)
           < lax.broadcasted_iota(jnp.int32, (tm, tm), 1)).astype(BF16)
    carry = carry_ref[:, 0:1]
    before = jnp.dot(onehot.astype(BF16), tri, preferred_element_type=F32) + carry
    for k in range(TOP_K):
        eidx_ref[k:k + 1, :] = idxs[k]
        gate_ref[k:k + 1, :] = exps[k] / denom
        rank_ref[k:k + 1, :] = jnp.sum(jnp.where(eio == idxs[k], before, 0.0), axis=0,
                                       keepdims=True).astype(jnp.int32)
    carry = carry + jnp.sum(onehot, axis=1, keepdims=True)
    carry_ref[...] = jnp.broadcast_to(carry, carry_ref.shape)
    counts_ref[...] = jnp.broadcast_to(carry, counts_ref.shape).astype(jnp.int32)


def _mix_call(x2d, mod, p, seq_len):
    n, d = x2d.shape
    steps_per_seq = seq_len // TM
    n_steps = n // TM
    row = lambda a: a.reshape(1, d)
    args = (
        x2d, mod, row(p["pre_g"]), row(p["post_g"]), p["w_in"], p["sc_conv"], p["sc_out"], p["cf_conv"],
        row(p["cf_conv_b"]), row(p["cf_ln_g"]), row(p["cf_ln_b"]), p["cf_out"], row(p["cf_out_b"]),
        p["pool_w"], row(p["pool_scale"]), p["w_o"], row(p["moe_pre_g"]), p["w_router_t"],
        p["b_router"].reshape(N_EXPERTS, 1),
    )
    in_specs = [
        pl.BlockSpec((TM, d), lambda i: (i, 0)),
        pl.BlockSpec((1, N_MOD, d), lambda i: (i // steps_per_seq, 0, 0)),
    ] + [_const_spec(a.shape) for a in args[2:]]
    tok_spec = pl.BlockSpec((TOP_K, TM), lambda i: (0, i))
    out_shape = (
        jax.ShapeDtypeStruct((n, d), F32),
        jax.ShapeDtypeStruct((n, d), F32),
        jax.ShapeDtypeStruct((TOP_K, n), jnp.int32),
        jax.ShapeDtypeStruct((TOP_K, n), F32),
        jax.ShapeDtypeStruct((TOP_K, n), jnp.int32),
        jax.ShapeDtypeStruct((N_EXPERTS, 128), jnp.int32),
    )
    out_specs = (
        pl.BlockSpec((TM, d), lambda i: (i, 0)),
        pl.BlockSpec((TM, d), lambda i: (i, 0)),
        tok_spec, tok_spec, tok_spec,
        pl.BlockSpec((N_EXPERTS, 128), lambda i: (0, 0)),
    )
    return pl.pallas_call(
        functools.partial(_mix_kernel, steps_per_seq=steps_per_seq),
        grid=(n_steps,),
        in_specs=in_specs,
        out_specs=out_specs,
        out_shape=out_shape,
        scratch_shapes=[
            pltpu.VMEM((SC_HALO + TM, d), F32),
            pltpu.VMEM((CF_HALO + TM, d), F32),
            pltpu.VMEM((POOL_HALO + TM, d), F32),
            pltpu.VMEM((TM, d), F32),
            pltpu.VMEM((N_EXPERTS, 128), F32),
        ],
        compiler_params=pltpu.CompilerParams(dimension_semantics=("arbitrary",),
                                             vmem_limit_bytes=VMEM_LIMIT),
        name="mixer_router",
    )(*args)


def _expert_kernel(be_ref, nb_ref, tok_hbm, h2_hbm, wgu_ref, bgu_ref, wd_ref, bd_ref, ys_ref,
                   idx_smem, xbuf, isem, rsem):
    i = pl.program_id(0)
    nb = nb_ref[0]
    slot = i % 2
    d_ff = wd_ref.shape[1]

    def idx_copy(blk, s):
        return pltpu.make_async_copy(tok_hbm.at[pl.ds(blk * BM, BM)], idx_smem.at[s], isem.at[s])

    def issue_rows(s):
        def body(r, carry):
            tok = idx_smem[s, r]
            pltpu.make_async_copy(h2_hbm.at[pl.ds(tok, 1)], xbuf.at[s, pl.ds(r, 1)], rsem.at[s]).start()
            return carry
        lax.fori_loop(0, BM, body, 0, unroll=8)

    def wait_rows(s):
        pltpu.make_async_copy(h2_hbm.at[pl.ds(0, BM)], xbuf.at[s], rsem.at[s]).wait()

    @pl.when(i == 0)
    def _():
        idx_copy(0, 0).start()
        idx_copy(0, 0).wait()
        issue_rows(0)

        @pl.when(nb > 1)
        def _():
            idx_copy(1, 1).start()

    @pl.when(i + 1 < nb)
    def _():
        idx_copy(i + 1, 1 - slot).wait()
        issue_rows(1 - slot)

        @pl.when(i + 2 < nb)
        def _():
            idx_copy(i + 2, slot).start()

    @pl.when(i < nb)
    def _():
        wait_rows(slot)
        xb = xbuf[slot].astype(BF16)
        hg = jnp.dot(xb, wgu_ref[0], preferred_element_type=F32) + bgu_ref[0]
        gt = jnp.minimum(hg[:, :d_ff], SWIGLU_LIMIT)
        up = jnp.clip(hg[:, d_ff:], -SWIGLU_LIMIT, SWIGLU_LIMIT)
        act = (up + 1.0) * gt * _sigmoid(gt * SWIGLU_ALPHA)
        ys_ref[...] = jnp.dot(act.astype(BF16), wd_ref[0], preferred_element_type=F32) + bd_ref[0]

    @pl.when(i >= nb)
    def _():
        ys_ref[...] = jnp.zeros(ys_ref.shape, F32)


def _expert_call(block_e, n_blocks, pad_tok, h2, w_gu, b_gu, w_down, b_down):
    n, d = h2.shape
    n_exp, _, two_ff = w_gu.shape
    d_ff = two_ff // 2
    nb_max = pad_tok.shape[0] // BM
    grid_spec = pltpu.PrefetchScalarGridSpec(
        num_scalar_prefetch=2,
        grid=(nb_max,),
        in_specs=[
            pl.BlockSpec(memory_space=pl.ANY),
            pl.BlockSpec(memory_space=pl.ANY),
            pl.BlockSpec((1, d, two_ff), lambda i, be, nb: (be[i], 0, 0)),
            pl.BlockSpec((1, 1, two_ff), lambda i, be, nb: (be[i], 0, 0)),
            pl.BlockSpec((1, d_ff, d), lambda i, be, nb: (be[i], 0, 0)),
            pl.BlockSpec((1, 1, d), lambda i, be, nb: (be[i], 0, 0)),
        ],
        out_specs=pl.BlockSpec((BM, d), lambda i, be, nb: (i, 0)),
        scratch_shapes=[
            pltpu.SMEM((2, BM), jnp.int32),
            pltpu.VMEM((2, BM, d), F32),
            pltpu.SemaphoreType.DMA((2,)),
            pltpu.SemaphoreType.DMA((2,)),
        ],
    )
    return pl.pallas_call(
        _expert_kernel,
        grid_spec=grid_spec,
        out_shape=jax.ShapeDtypeStruct((nb_max * BM, d), F32),
        compiler_params=pltpu.CompilerParams(dimension_semantics=("arbitrary",),
                                             vmem_limit_bytes=VMEM_LIMIT),
        name="experts",
    )(block_e, n_blocks, pad_tok, h2, w_gu, b_gu.reshape(n_exp, 1, two_ff), w_down,
      b_down.reshape(n_exp, 1, d))


def _comb_kernel(dest_hbm, ys_hbm, x1_ref, gate_ref, mod_ref, post_g, x2_ref,
                 idx_smem, gbuf, isem, rsem, *, n_steps):
    i = pl.program_id(0)
    slot = i % 2
    n_idx = TOP_K * TM

    def idx_copy(step, s):
        return pltpu.make_async_copy(dest_hbm.at[pl.ds(step * n_idx, n_idx)], idx_smem.at[s], isem.at[s])

    def issue_rows(s):
        for k in range(TOP_K):
            def body(r, carry, k=k):
                row = idx_smem[s, k * TM + r]
                pltpu.make_async_copy(ys_hbm.at[pl.ds(row, 1)], gbuf.at[s, k, pl.ds(r, 1)],
                                      rsem.at[s]).start()
                return carry
            lax.fori_loop(0, TM, body, 0, unroll=8)

    def wait_rows(s):
        for k in range(TOP_K):
            pltpu.make_async_copy(ys_hbm.at[pl.ds(0, TM)], gbuf.at[s, k], rsem.at[s]).wait()

    @pl.when(i == 0)
    def _():
        idx_copy(0, 0).start()
        idx_copy(0, 0).wait()
        issue_rows(0)
        if n_steps > 1:
            idx_copy(1, 1).start()

    @pl.when(i + 1 < n_steps)
    def _():
        idx_copy(i + 1, 1 - slot).wait()
        issue_rows(1 - slot)

        @pl.when(i + 2 < n_steps)
        def _():
            idx_copy(i + 2, slot).start()

    wait_rows(slot)
    gate = gate_ref[...]
    y = gate[:, 0:1] * gbuf[slot, 0]
    for k in range(1, TOP_K):
        y = y + gate[:, k:k + 1] * gbuf[slot, k]
    gate2 = mod_ref[0, 5:6, :]
    x2_ref[...] = x1_ref[...] + gate2 * _rms(y, post_g[...])


def _comb_call(dest_flat, ys, x1, gate_nk, mod, post_g, seq_len):
    n, d = x1.shape
    steps_per_seq = seq_len // TM
    n_steps = n // TM
    return pl.pallas_call(
        functools.partial(_comb_kernel, n_steps=n_steps),
        grid=(n_steps,),
        in_specs=[
            pl.BlockSpec(memory_space=pl.ANY),
            pl.BlockSpec(memory_space=pl.ANY),
            pl.BlockSpec((TM, d), lambda i: (i, 0)),
            pl.BlockSpec((TM, TOP_K), lambda i: (i, 0)),
            pl.BlockSpec((1, N_MOD, d), lambda i: (i // steps_per_seq, 0, 0)),
            pl.BlockSpec((1, d), lambda i: (0, 0)),
        ],
        out_specs=pl.BlockSpec((TM, d), lambda i: (i, 0)),
        out_shape=jax.ShapeDtypeStruct((n, d), F32),
        scratch_shapes=[
            pltpu.SMEM((2, TOP_K * TM), jnp.int32),
            pltpu.VMEM((2, TOP_K, TM, d), F32),
            pltpu.SemaphoreType.DMA((2,)),
            pltpu.SemaphoreType.DMA((2,)),
        ],
        compiler_params=pltpu.CompilerParams(dimension_semantics=("arbitrary",),
                                             vmem_limit_bytes=VMEM_LIMIT),
        name="combine",
    )(dest_flat, ys, x1, gate_nk, mod, post_g.reshape(1, d))


def _routing_tables(eidx, rank, counts):
    n = eidx.shape[1]
    n_pad = n * TOP_K + N_EXPERTS * BM
    nb_max = n_pad // BM
    padded = (counts + BM - 1) // BM * BM
    pend = jnp.cumsum(padded)
    pstart = pend - padded
    dest = pstart[eidx] + rank
    tok = jnp.broadcast_to(jnp.arange(n, dtype=jnp.int32)[None, :], (TOP_K, n))
    pad_tok = jnp.zeros((n_pad,), jnp.int32).at[dest.reshape(-1)].set(tok.reshape(-1))
    block_e = jnp.minimum(
        jnp.searchsorted(pend, jnp.arange(nb_max, dtype=jnp.int32) * BM, side="right"),
        N_EXPERTS - 1).astype(jnp.int32)
    n_blocks = (pend[-1:] // BM).astype(jnp.int32)
    dest_flat = dest.reshape(TOP_K, n // TM, TM).transpose(1, 0, 2).reshape(-1)
    return dest_flat, pad_tok, block_e, n_blocks


def kernel(x, c, ada_w, ada_b, mix_pre_g, mix_post_g, w_in, sc_conv, sc_out, cf_conv, cf_conv_b, cf_ln_g, cf_ln_b, cf_out, cf_out_b, pool_w, pool_scale, w_o, moe_pre_g, moe_post_g, w_router, b_router, w_gu, b_gu, w_down, b_down):
    bsz, seq_len, d = x.shape
    depth = ada_w.shape[0]
    assert seq_len % TM == 0 and d % COL_CHUNK == 0 and TM % ROW_CHUNK == 0
    mods = _ada_call(c, ada_w, ada_b)
    xs = x.reshape(bsz * seq_len, d)
    for l in range(depth):
        p = dict(
            pre_g=mix_pre_g[l], post_g=mix_post_g[l], w_in=w_in[l].astype(BF16), sc_conv=sc_conv[l],
            sc_out=sc_out[l].astype(BF16), cf_conv=cf_conv[l], cf_conv_b=cf_conv_b[l], cf_ln_g=cf_ln_g[l],
            cf_ln_b=cf_ln_b[l], cf_out=cf_out[l].astype(BF16), cf_out_b=cf_out_b[l],
            pool_w=pool_w[l].astype(BF16), pool_scale=pool_scale[l], w_o=w_o[l].astype(BF16),
            moe_pre_g=moe_pre_g[l], w_router_t=w_router[l].T, b_router=b_router[l],
        )
        x1, h2, eidx, gate, rank, counts = _mix_call(xs, mods[l], p, seq_len)
        dest_flat, pad_tok, block_e, n_blocks = _routing_tables(eidx, rank, counts[:, 0])
        ys = _expert_call(block_e, n_blocks, pad_tok, h2, w_gu[l].astype(BF16), b_gu[l],
                          w_down[l].astype(BF16), b_down[l])
        xs = _comb_call(dest_flat, ys, x1, gate.T, mods[l], moe_post_g[l], seq_len)
    return xs.reshape(bsz, seq_len, d)
```

```python
import functools

import jax
import jax.numpy as jnp
from jax import lax
from jax.experimental import pallas as pl
from jax.experimental.pallas import tpu as pltpu

F32 = jnp.float32
BF16 = jnp.bfloat16

N_EXPERTS = 32
TOP_K = 4
N_MOD = 6
SC_WIDTH = 3
CF_WIDTH = 31
POOL_WINDOWS = (2, 4, 8, 16)
SWIGLU_LIMIT = 7.0
SWIGLU_ALPHA = 1.702
NORM_EPS = 1e-6
LN_EPS = 1e-5

SUBLANES = 8
TM = 256
BM = 256
SC_HALO = 8
CF_HALO = 32
POOL_HALO = 16
ROW_CHUNK = 32
COL_CHUNK = 256
ROUTE_CHUNK = 2048
VMEM_LIMIT = 60 * 1024 * 1024


def _rms(x, g):
    return x * lax.rsqrt(jnp.mean(x * x, axis=-1, keepdims=True) + NORM_EPS) * g


def _sigmoid(x):
    return 1.0 / (1.0 + jnp.exp(-x))


def _const_spec(shape):
    nd = len(shape)
    return pl.BlockSpec(shape, lambda *_: (0,) * nd, pipeline_mode=pl.Buffered(1))


def _ada_kernel(c_ref, w_ref, b_ref, o_ref):
    c = c_ref[...]
    c_act = c * _sigmoid(c)
    o_ref[0, 0] = jnp.dot(c_act, w_ref[0], preferred_element_type=F32,
                          precision=lax.Precision.HIGHEST) + b_ref[0, 0]


def _ada_call(c, ada_w, ada_b):
    depth, d, _ = ada_w.shape
    bsz = c.shape[0]
    ada_b4 = ada_b.reshape(depth, N_MOD, 1, d)
    out = pl.pallas_call(
        _ada_kernel,
        grid=(depth, N_MOD),
        in_specs=[
            pl.BlockSpec((bsz, d), lambda l, j: (0, 0)),
            pl.BlockSpec((1, d, d), lambda l, j: (l, 0, j)),
            pl.BlockSpec((1, 1, 1, d), lambda l, j: (l, j, 0, 0)),
        ],
        out_specs=pl.BlockSpec((1, 1, bsz, d), lambda l, j: (l, j, 0, 0)),
        out_shape=jax.ShapeDtypeStruct((depth, N_MOD, bsz, d), F32),
        name="ada_mod",
    )(c, ada_w, ada_b4)
    return jnp.transpose(out, (0, 2, 1, 3))


def _causal_conv(buf_ref, w_ref, width, halo, out_ref, bias_ref=None):
    tm, d = out_ref.shape
    for c0 in range(0, d, COL_CHUNK):
        cols = slice(c0, c0 + COL_CHUNK)
        for r0 in range(0, tm, ROW_CHUNK):
            acc = None
            for k in range(width):
                start = halo + r0 - (width - 1) + k
                term = w_ref[k:k + 1, cols] * buf_ref[start:start + ROW_CHUNK, cols]
                acc = term if acc is None else acc + term
            if bias_ref is not None:
                acc = acc + bias_ref[:, cols]
            out_ref[r0:r0 + ROW_CHUNK, cols] = acc


def _mix_kernel(x_ref, mod_ref, pre_g, post_g, w_in, sc_conv, sc_out, cf_conv, cf_conv_b, cf_ln_g,
                cf_ln_b, cf_out, cf_out_b, pool_w, pool_scale, w_o, moe_pre_g, w_router_t, b_router,
                x1_ref, h2_ref, eidx_ref, gate_ref, rank_ref, counts_ref,
                cx_buf, v_buf, u_buf, tmp_buf, carry_ref, *, steps_per_seq):
    i = pl.program_id(0)
    t = i % steps_per_seq
    tm, d = x_ref.shape

    @pl.when(t == 0)
    def _():
        cx_buf[0:SC_HALO, :] = jnp.zeros((SC_HALO, d), F32)
        v_buf[0:CF_HALO, :] = jnp.zeros((CF_HALO, d), F32)
        u_buf[0:POOL_HALO, :] = jnp.zeros((POOL_HALO, d), F32)

    @pl.when(i == 0)
    def _():
        carry_ref[...] = jnp.zeros(carry_ref.shape, F32)

    x = x_ref[...]
    shift1, scale1, gate1 = mod_ref[0, 0:1, :], mod_ref[0, 1:2, :], mod_ref[0, 2:3, :]
    shift2, scale2 = mod_ref[0, 3:4, :], mod_ref[0, 4:5, :]

    hb = (_rms(x, pre_g[...]) * (1.0 + scale1) + shift1).astype(BF16)

    def proj(j):
        return jnp.dot(hb, w_in[:, j * d:(j + 1) * d], preferred_element_type=F32)

    cx_buf[SC_HALO:SC_HALO + tm, :] = proj(1) * proj(2)
    _causal_conv(cx_buf, sc_conv, SC_WIDTH, SC_HALO, tmp_buf)
    y_a = jnp.dot((proj(0) * tmp_buf[...]).astype(BF16), sc_out[...], preferred_element_type=F32)
    cx_buf[0:SC_HALO, :] = cx_buf[tm:tm + SC_HALO, :]
    merged = _sigmoid(proj(6)) * y_a

    v_buf[CF_HALO:CF_HALO + tm, :] = proj(3) * _sigmoid(proj(4))
    _causal_conv(v_buf, cf_conv, CF_WIDTH, CF_HALO, tmp_buf, cf_conv_b)
    v_buf[0:CF_HALO, :] = v_buf[tm:tm + CF_HALO, :]
    v = tmp_buf[...]
    mu = jnp.mean(v, axis=-1, keepdims=True)
    vc = v - mu
    var = jnp.mean(vc * vc, axis=-1, keepdims=True)
    vn = vc * lax.rsqrt(var + LN_EPS) * cf_ln_g[...] + cf_ln_b[...]
    vn = vn * _sigmoid(vn)
    y_b = jnp.dot(vn.astype(BF16), cf_out[...], preferred_element_type=F32) + cf_out_b[...]
    merged = merged + _sigmoid(proj(7)) * y_b

    u = proj(5)
    u_buf[POOL_HALO:POOL_HALO + tm, :] = u
    pos = (t * tm + 1 + lax.broadcasted_iota(jnp.int32, (tm, 1), 0)).astype(F32)
    group = d // len(POOL_WINDOWS)
    y_c_parts = []
    for g, w in enumerate(POOL_WINDOWS):
        cols = slice(g * group, (g + 1) * group)
        s = u[:, cols]
        for j in range(1, w):
            s = s + u_buf[POOL_HALO - j:POOL_HALO - j + tm, cols]
        p = s / jnp.minimum(pos, float(w)) - u[:, cols]
        y_c_parts.append(jnp.dot(p.astype(BF16), pool_w[g], preferred_element_type=F32))
    u_buf[0:POOL_HALO, :] = u_buf[tm:tm + POOL_HALO, :]
    y_c = jnp.concatenate(y_c_parts, axis=-1) * pool_scale[...]
    merged = merged + _sigmoid(proj(8)) * y_c

    y = jnp.dot(merged.astype(BF16), w_o[...], preferred_element_type=F32)
    x1 = x + gate1 * _rms(y, post_g[...])
    x1_ref[...] = x1

    h2 = _rms(x1, moe_pre_g[...]) * (1.0 + scale2) + shift2
    h2_ref[...] = h2
    logits = lax.dot_general(w_router_t[...], h2, (((1,), (1,)), ((), ())),
                             preferred_element_type=F32,
                             precision=lax.Precision.HIGHEST) + b_router[...]
    eio = lax.broadcasted_iota(jnp.int32, (N_EXPERTS, tm), 0)
    neg_inf = jnp.float32(-jnp.inf)
    vals, idxs = [], []
    for _ in range(TOP_K):
        m = jnp.max(logits, axis=0, keepdims=True)
        idx = jnp.min(jnp.where(logits == m, eio, N_EXPERTS), axis=0, keepdims=True)
        vals.append(m)
        idxs.append(idx)
        logits = jnp.where(eio == idx, neg_inf, logits)
    exps = [jnp.exp(v - vals[0]) for v in vals]
    denom = exps[0] + exps[1] + exps[2] + exps[3]
    onehot = jnp.zeros((N_EXPERTS, tm), F32)
    for idx in idxs:
        onehot = onehot + (eio == idx).astype(F32)
    tri = (lax.broadcasted_iota(jnp.int32, (tm, tm), 0)
           < lax.broadcasted_iota(jnp.int32, (tm, tm), 1)).astype(BF16)
    carry = carry_ref[:, 0:1]
    before = jnp.dot(onehot.astype(BF16), tri, preferred_element_type=F32) + carry
    for k in range(TOP_K):
        eidx_ref[k:k + 1, :] = idxs[k]
        gate_ref[k:k + 1, :] = exps[k] / denom
        rank_ref[k:k + 1, :] = jnp.sum(jnp.where(eio == idxs[k], before, 0.0), axis=0,
                                       keepdims=True).astype(jnp.int32)
    carry = carry + jnp.sum(onehot, axis=1, keepdims=True)
    carry_ref[...] = jnp.broadcast_to(carry, carry_ref.shape)
    counts_ref[...] = jnp.broadcast_to(carry, counts_ref.shape).astype(jnp.int32)


def _mix_call(x2d, mod, p, seq_len):
    n, d = x2d.shape
    steps_per_seq = seq_len // TM
    n_steps = n // TM
    row = lambda a: a.reshape(1, d)
    args = (
        x2d, mod, row(p["pre_g"]), row(p["post_g"]), p["w_in"], p["sc_conv"], p["sc_out"], p["cf_conv"],
        row(p["cf_conv_b"]), row(p["cf_ln_g"]), row(p["cf_ln_b"]), p["cf_out"], row(p["cf_out_b"]),
        p["pool_w"], row(p["pool_scale"]), p["w_o"], row(p["moe_pre_g"]), p["w_router_t"],
        p["b_router"].reshape(N_EXPERTS, 1),
    )
    in_specs = [
        pl.BlockSpec((TM, d), lambda i: (i, 0)),
        pl.BlockSpec((1, N_MOD, d), lambda i: (i // steps_per_seq, 0, 0)),
    ] + [_const_spec(a.shape) for a in args[2:]]
    tok_spec = pl.BlockSpec((TOP_K, TM), lambda i: (0, i))
    out_shape = (
        jax.ShapeDtypeStruct((n, d), F32),
        jax.ShapeDtypeStruct((n, d), F32),
        jax.ShapeDtypeStruct((TOP_K, n), jnp.int32),
        jax.ShapeDtypeStruct((TOP_K, n), F32),
        jax.ShapeDtypeStruct((TOP_K, n), jnp.int32),
        jax.ShapeDtypeStruct((N_EXPERTS, 128), jnp.int32),
    )
    out_specs = (
        pl.BlockSpec((TM, d), lambda i: (i, 0)),
        pl.BlockSpec((TM, d), lambda i: (i, 0)),
        tok_spec, tok_spec, tok_spec,
        pl.BlockSpec((N_EXPERTS, 128), lambda i: (0, 0)),
    )
    return pl.pallas_call(
        functools.partial(_mix_kernel, steps_per_seq=steps_per_seq),
        grid=(n_steps,),
        in_specs=in_specs,
        out_specs=out_specs,
        out_shape=out_shape,
        scratch_shapes=[
            pltpu.VMEM((SC_HALO + TM, d), F32),
            pltpu.VMEM((CF_HALO + TM, d), F32),
            pltpu.VMEM((POOL_HALO + TM, d), F32),
            pltpu.VMEM((TM, d), F32),
            pltpu.VMEM((N_EXPERTS, 128), F32),
        ],
        compiler_params=pltpu.CompilerParams(dimension_semantics=("arbitrary",),
                                             vmem_limit_bytes=VMEM_LIMIT),
        name="mixer_router",
    )(*args)


def _expert_kernel(be_ref, nb_ref, tok_hbm, h2_hbm, wgu_ref, bgu_ref, wd_ref, bd_ref, ys_ref,
                   idx_smem, xbuf, wgu_bf, wd_bf, isem, rsem, *, nb_max):
    i = pl.program_id(0)
    nb = nb_ref[0]
    slot = i % 2
    d_ff = wd_bf.shape[0]

    def idx_copy(blk, s):
        return pltpu.make_async_copy(tok_hbm.at[pl.ds(blk * BM, BM)], idx_smem.at[s], isem.at[s])

    def row_copy(s, r):
        tok = idx_smem[s, r]
        return pltpu.make_async_copy(h2_hbm.at[pl.ds(tok, 1)], xbuf.at[s, pl.ds(r, 1)], rsem.at[s])

    def wait_rows(s):
        pltpu.make_async_copy(h2_hbm.at[pl.ds(0, BM)], xbuf.at[s], rsem.at[s]).wait()

    @pl.when(i == 0)
    def _():
        idx_copy(0, 0).start()
        idx_copy(0, 0).wait()

        def body(r, carry):
            row_copy(0, r).start()
            return carry
        lax.fori_loop(0, BM, body, 0, unroll=8)
        idx_copy(1, 1).start()

    prev_e = be_ref[jnp.maximum(i - 1, 0)]

    @pl.when((i < nb) & ((i == 0) | (be_ref[i] != prev_e)))
    def _():
        wgu_bf[...] = wgu_ref[0, 0].astype(BF16)
        wd_bf[...] = wd_ref[0, 0].astype(BF16)

    @pl.when(i < nb)
    def _():
        wait_rows(slot)
        idx_copy(i + 1, 1 - slot).wait()
        for r in range(BM):
            row_copy(1 - slot, r).start()
        xb = xbuf[slot].astype(BF16)
        hg = jnp.dot(xb, wgu_bf[...], preferred_element_type=F32) + bgu_ref[0, 0]
        gt = jnp.minimum(hg[:, :d_ff], SWIGLU_LIMIT)
        up = jnp.clip(hg[:, d_ff:], -SWIGLU_LIMIT, SWIGLU_LIMIT)
        act = (up + 1.0) * gt * _sigmoid(gt * SWIGLU_ALPHA)
        ys_ref[...] = jnp.dot(act.astype(BF16), wd_bf[...], preferred_element_type=F32) + bd_ref[0, 0]

        @pl.when(i + 2 < nb_max)
        def _():
            idx_copy(i + 2, slot).start()

    @pl.when(i >= nb)
    def _():
        @pl.when(i == nb)
        def _():
            wait_rows(slot)

            @pl.when(i + 1 < nb_max)
            def _():
                idx_copy(i + 1, 1 - slot).wait()

        ys_ref[...] = jnp.zeros(ys_ref.shape, F32)


def _expert_call(layer, block_e, n_blocks, pad_tok, h2, w_gu, b_gu, w_down, b_down):
    n, d = h2.shape
    depth, n_exp, _, two_ff = w_gu.shape
    d_ff = two_ff // 2
    nb_max = pad_tok.shape[0] // BM
    grid_spec = pltpu.PrefetchScalarGridSpec(
        num_scalar_prefetch=2,
        grid=(nb_max,),
        in_specs=[
            pl.BlockSpec(memory_space=pl.ANY),
            pl.BlockSpec(memory_space=pl.ANY),
            pl.BlockSpec((1, 1, d, two_ff), lambda i, be, nb: (layer, be[i], 0, 0)),
            pl.BlockSpec((1, 1, 1, two_ff), lambda i, be, nb: (layer, be[i], 0, 0)),
            pl.BlockSpec((1, 1, d_ff, d), lambda i, be, nb: (layer, be[i], 0, 0)),
            pl.BlockSpec((1, 1, 1, d), lambda i, be, nb: (layer, be[i], 0, 0)),
        ],
        out_specs=pl.BlockSpec((BM, d), lambda i, be, nb: (i, 0)),
        scratch_shapes=[
            pltpu.SMEM((2, BM), jnp.int32),
            pltpu.VMEM((2, BM, d), F32),
            pltpu.VMEM((d, two_ff), BF16),
            pltpu.VMEM((d_ff, d), BF16),
            pltpu.SemaphoreType.DMA((2,)),
            pltpu.SemaphoreType.DMA((2,)),
        ],
    )
    return pl.pallas_call(
        functools.partial(_expert_kernel, nb_max=nb_max),
        grid_spec=grid_spec,
        out_shape=jax.ShapeDtypeStruct((nb_max * BM, d), F32),
        compiler_params=pltpu.CompilerParams(dimension_semantics=("arbitrary",),
                                             vmem_limit_bytes=VMEM_LIMIT),
        name="experts",
    )(block_e, n_blocks, pad_tok, h2, w_gu, b_gu.reshape(depth, n_exp, 1, two_ff), w_down,
      b_down.reshape(depth, n_exp, 1, d))


class _CopyGroup:
    def __init__(self, copies):
        self.copies = copies

    def start(self):
        for c in self.copies:
            c.start()

    def wait(self):
        for c in self.copies:
            c.wait()


def _comb_kernel(dest_hbm, ys_hbm, x1_ref, gate_ref, mod_ref, post_g, x2_ref,
                 idx_smem, gbuf, isem, rsem, *, n_steps):
    i = pl.program_id(0)
    slot = i % 2
    n_tok = n_steps * TM

    def idx_copy(step, s):
        return _CopyGroup([
            pltpu.make_async_copy(dest_hbm.at[pl.ds(k * n_tok + step * TM, TM)],
                                  idx_smem.at[s, pl.ds(k * TM, TM)], isem.at[s])
            for k in range(TOP_K)])

    def issue_rows(s):
        for k in range(TOP_K):
            def body(r, carry, k=k):
                row = idx_smem[s, k * TM + r]
                pltpu.make_async_copy(ys_hbm.at[pl.ds(row, 1)], gbuf.at[s, k, pl.ds(r, 1)],
                                      rsem.at[s]).start()
                return carry
            lax.fori_loop(0, TM, body, 0, unroll=8)

    def wait_rows(s):
        for k in range(TOP_K):
            pltpu.make_async_copy(ys_hbm.at[pl.ds(0, TM)], gbuf.at[s, k], rsem.at[s]).wait()

    @pl.when(i == 0)
    def _():
        idx_copy(0, 0).start()
        idx_copy(0, 0).wait()
        issue_rows(0)
        if n_steps > 1:
            idx_copy(1, 1).start()

    @pl.when(i + 1 < n_steps)
    def _():
        idx_copy(i + 1, 1 - slot).wait()
        issue_rows(1 - slot)

        @pl.when(i + 2 < n_steps)
        def _():
            idx_copy(i + 2, slot).start()

    wait_rows(slot)
    gate = gate_ref[...]
    y = gate[:, 0:1] * gbuf[slot, 0]
    for k in range(1, TOP_K):
        y = y + gate[:, k:k + 1] * gbuf[slot, k]
    gate2 = mod_ref[0, 5:6, :]
    x2_ref[...] = x1_ref[...] + gate2 * _rms(y, post_g[...])


def _comb_call(dest_flat, ys, x1, gate_nk, mod, post_g, seq_len):
    n, d = x1.shape
    steps_per_seq = seq_len // TM
    n_steps = n // TM
    return pl.pallas_call(
        functools.partial(_comb_kernel, n_steps=n_steps),
        grid=(n_steps,),
        in_specs=[
            pl.BlockSpec(memory_space=pl.ANY),
            pl.BlockSpec(memory_space=pl.ANY),
            pl.BlockSpec((TM, d), lambda i: (i, 0)),
            pl.BlockSpec((TM, TOP_K), lambda i: (i, 0)),
            pl.BlockSpec((1, N_MOD, d), lambda i: (i // steps_per_seq, 0, 0)),
            pl.BlockSpec((1, d), lambda i: (0, 0)),
        ],
        out_specs=pl.BlockSpec((TM, d), lambda i: (i, 0)),
        out_shape=jax.ShapeDtypeStruct((n, d), F32),
        scratch_shapes=[
            pltpu.SMEM((2, TOP_K * TM), jnp.int32),
            pltpu.VMEM((2, TOP_K, TM, d), F32),
            pltpu.SemaphoreType.DMA((2,)),
            pltpu.SemaphoreType.DMA((2,)),
        ],
        compiler_params=pltpu.CompilerParams(dimension_semantics=("arbitrary",),
                                             vmem_limit_bytes=VMEM_LIMIT),
        name="combine",
    )(dest_flat, ys, x1, gate_nk, mod, post_g.reshape(1, d))


def _route_kernel(pstart_ref, pvalid_ref, pend_ref, eidx_hbm, rank_hbm, dest_hbm, inv_ref,
                  e_s, r_s, d_s, in_sem, out_sem, *, n_tok):
    n_asg = TOP_K * n_tok
    n_pad = inv_ref.shape[0]
    n_chunks = n_asg // ROUTE_CHUNK

    def in_copy(c, s):
        src = pl.ds(c * ROUTE_CHUNK, ROUTE_CHUNK)
        return _CopyGroup([pltpu.make_async_copy(eidx_hbm.at[src], e_s.at[s], in_sem.at[0, s]),
                           pltpu.make_async_copy(rank_hbm.at[src], r_s.at[s], in_sem.at[1, s])])

    def out_copy(c, s):
        return pltpu.make_async_copy(d_s.at[s], dest_hbm.at[pl.ds(c * ROUTE_CHUNK, ROUTE_CHUNK)],
                                     out_sem.at[s])

    c = pl.program_id(0)
    s = c % 2

    @pl.when(c == 0)
    def _():
        in_copy(0, 0).start()

        def zero_slot(p, carry):
            inv_ref[p] = 0
            return carry

        def zero_expert(e, carry):
            lax.fori_loop(pvalid_ref[e], pend_ref[e], zero_slot, 0)
            return carry

        lax.fori_loop(0, N_EXPERTS, zero_expert, 0)
        lax.fori_loop(pend_ref[N_EXPERTS - 1], n_pad, zero_slot, 0)

    @pl.when(c + 1 < n_chunks)
    def _():
        in_copy(c + 1, 1 - s).start()

    in_copy(c, s).wait()

    @pl.when(c >= 2)
    def _():
        out_copy(c - 2, s).wait()

    tok0 = (c * ROUTE_CHUNK) % n_tok

    def body(j, carry):
        p = pstart_ref[e_s[s, j]] + r_s[s, j]
        d_s[s, j] = p
        inv_ref[p] = tok0 + j
        return carry

    lax.fori_loop(0, ROUTE_CHUNK, body, 0, unroll=8)
    out_copy(c, s).start()

    @pl.when(c == n_chunks - 1)
    def _():
        if n_chunks >= 2:
            out_copy(c - 1, 1 - s).wait()
        out_copy(c, s).wait()


def _routing_tables(eidx, rank, counts):
    n = eidx.shape[1]
    n_asg = n * TOP_K
    n_pad = n_asg + N_EXPERTS * BM
    nb_max = n_pad // BM
    padded = (counts + BM - 1) // BM * BM
    pend = jnp.cumsum(padded).astype(jnp.int32)
    pstart = pend - padded
    n_blocks = pend[-1:] // BM
    blk = jnp.arange(nb_max, dtype=jnp.int32)
    block_e = jnp.minimum(jnp.sum((pend[None, :] <= blk[:, None] * BM).astype(jnp.int32), axis=1),
                          N_EXPERTS - 1)
    last_e = jnp.sum(jnp.where(blk == n_blocks[0] - 1, block_e, 0))
    block_e = jnp.where(blk < n_blocks[0], block_e, last_e).astype(jnp.int32)
    smem = pl.BlockSpec(memory_space=pltpu.SMEM)
    hbm = pl.BlockSpec(memory_space=pl.ANY)
    dest_flat, pad_tok = pl.pallas_call(
        functools.partial(_route_kernel, n_tok=n),
        grid=(n_asg // ROUTE_CHUNK,),
        in_specs=[smem, smem, smem, hbm, hbm],
        out_specs=(hbm, smem),
        out_shape=(jax.ShapeDtypeStruct((n_asg,), jnp.int32),
                   jax.ShapeDtypeStruct((n_pad,), jnp.int32)),
        scratch_shapes=[
            pltpu.SMEM((2, ROUTE_CHUNK), jnp.int32),
            pltpu.SMEM((2, ROUTE_CHUNK), jnp.int32),
            pltpu.SMEM((2, ROUTE_CHUNK), jnp.int32),
            pltpu.SemaphoreType.DMA((2, 2)),
            pltpu.SemaphoreType.DMA((2,)),
        ],
        compiler_params=pltpu.CompilerParams(dimension_semantics=("arbitrary",)),
        name="route_tables",
    )(pstart, pstart + counts, pend, eidx.reshape(-1), rank.reshape(-1))
    return dest_flat, pad_tok, block_e, n_blocks


def kernel(x, c, ada_w, ada_b, mix_pre_g, mix_post_g, w_in, sc_conv, sc_out, cf_conv, cf_conv_b, cf_ln_g, cf_ln_b, cf_out, cf_out_b, pool_w, pool_scale, w_o, moe_pre_g, moe_post_g, w_router, b_router, w_gu, b_gu, w_down, b_down):
    bsz, seq_len, d = x.shape
    depth = ada_w.shape[0]
    assert seq_len % TM == 0 and d % COL_CHUNK == 0 and TM % ROW_CHUNK == 0
    n_tok = bsz * seq_len
    assert n_tok % ROUTE_CHUNK == 0 and (n_tok * TOP_K) % BM == 0
    mods = _ada_call(c, ada_w, ada_b)
    xs = x.reshape(bsz * seq_len, d)
    for l in range(depth):
        p = dict(
            pre_g=mix_pre_g[l], post_g=mix_post_g[l], w_in=w_in[l].astype(BF16), sc_conv=sc_conv[l],
            sc_out=sc_out[l].astype(BF16), cf_conv=cf_conv[l], cf_conv_b=cf_conv_b[l], cf_ln_g=cf_ln_g[l],
            cf_ln_b=cf_ln_b[l], cf_out=cf_out[l].astype(BF16), cf_out_b=cf_out_b[l],
            pool_w=pool_w[l].astype(BF16), pool_scale=pool_scale[l], w_o=w_o[l].astype(BF16),
            moe_pre_g=moe_pre_g[l], w_router_t=w_router[l].T, b_router=b_router[l],
        )
        x1, h2, eidx, gate, rank, counts = _mix_call(xs, mods[l], p, seq_len)
        dest_flat, pad_tok, block_e, n_blocks = _routing_tables(eidx, rank, counts[:, 0])
        ys = _expert_call(l, block_e, n_blocks, pad_tok, h2, w_gu, b_gu, w_down, b_down)
        xs = _comb_call(dest_flat, ys, x1, gate.T, mods[l], moe_post_g[l], seq_len)
    return xs.reshape(bsz, seq_len, d)
```

```python
import functools

import jax
import jax.numpy as jnp
from jax import lax
from jax.experimental import pallas as pl
from jax.experimental.pallas import tpu as pltpu

F32 = jnp.float32
BF16 = jnp.bfloat16

N_EXPERTS = 32
TOP_K = 4
N_MOD = 6
SC_WIDTH = 3
CF_WIDTH = 31
POOL_WINDOWS = (2, 4, 8, 16)
SWIGLU_LIMIT = 7.0
SWIGLU_ALPHA = 1.702
NORM_EPS = 1e-6
LN_EPS = 1e-5

SUBLANES = 8
LANES = 128
MXU_DEPTH = 256
TM = 256
BM = 256
SC_HALO = 8
CF_HALO = 32
POOL_HALO = 16
ROW_CHUNK = 32
COL_CHUNK = 256
ROUTE_ROWS = 16
DUMP_BLOCKS = 3
VMEM_LIMIT = 60 * 1024 * 1024


def _rms(x, g):
    return x * lax.rsqrt(jnp.mean(x * x, axis=-1, keepdims=True) + NORM_EPS) * g


def _sigmoid(x):
    return 1.0 / (1.0 + jnp.exp(-x))


def _const_spec(shape):
    nd = len(shape)
    return pl.BlockSpec(shape, lambda *_: (0,) * nd, pipeline_mode=pl.Buffered(1))


def _load_row_tiled(ref, base, rows):
    return jnp.concatenate(
        [ref[pl.ds(base + s, rows, stride=SUBLANES), :] for s in range(SUBLANES)], axis=-1)


def _store_row_tiled(ref, base, value):
    rows = value.shape[0]
    for s in range(SUBLANES):
        ref[pl.ds(base + s, rows, stride=SUBLANES), :] = value[:, s * LANES:(s + 1) * LANES]


def _ada_kernel(c_ref, w_ref, b_ref, o_ref):
    c = c_ref[...]
    c_act = c * _sigmoid(c)
    o_ref[0, 0] = jnp.dot(c_act, w_ref[0], preferred_element_type=F32,
                          precision=lax.Precision.HIGHEST) + b_ref[0, 0]


def _ada_call(c, ada_w, ada_b):
    depth, d, _ = ada_w.shape
    bsz = c.shape[0]
    ada_b4 = ada_b.reshape(depth, N_MOD, 1, d)
    out = pl.pallas_call(
        _ada_kernel,
        grid=(depth, N_MOD),
        in_specs=[
            pl.BlockSpec((bsz, d), lambda l, j: (0, 0)),
            pl.BlockSpec((1, d, d), lambda l, j: (l, 0, j)),
            pl.BlockSpec((1, 1, 1, d), lambda l, j: (l, j, 0, 0)),
        ],
        out_specs=pl.BlockSpec((1, 1, bsz, d), lambda l, j: (l, j, 0, 0)),
        out_shape=jax.ShapeDtypeStruct((depth, N_MOD, bsz, d), F32),
        name="ada_mod",
    )(c, ada_w, ada_b4)
    return jnp.transpose(out, (0, 2, 1, 3))


def _combine(x1_ref, y4_refs, gate_ref, mod_ref, post_g):
    tm = x1_ref.shape[0]
    gate = gate_ref[...]
    y = None
    for k, y_ref in enumerate(y4_refs):
        term = gate[:, k:k + 1] * _load_row_tiled(y_ref, 0, tm)
        y = term if y is None else y + term
    return x1_ref[...] + mod_ref[0, 5:6, :] * _rms(y, post_g[...])


def _combine_specs(n, d, steps_per_seq):
    blocks_per_plane = n // TM
    specs = [pl.BlockSpec((TM, d), lambda i: (i, 0))]
    for k in range(TOP_K):
        specs.append(pl.BlockSpec((TM * SUBLANES, LANES),
                                  lambda i, k=k: (k * blocks_per_plane + i, 0)))
    specs += [
        pl.BlockSpec((TM, TOP_K), lambda i: (i, 0)),
        pl.BlockSpec((1, N_MOD, d), lambda i: (i // steps_per_seq, 0, 0)),
        pl.BlockSpec((1, d), lambda i: (0, 0)),
    ]
    return specs


def _final_kernel(x1_ref, y0, y1, y2, y3, gate_ref, mod_ref, post_g, out_ref):
    out_ref[...] = _combine(x1_ref, (y0, y1, y2, y3), gate_ref, mod_ref, post_g)


def _final_call(comb, seq_len):
    x1, y4, gate_nk, mod, post_g = comb
    n, d = x1.shape
    return pl.pallas_call(
        _final_kernel,
        grid=(n // TM,),
        in_specs=_combine_specs(n, d, seq_len // TM),
        out_specs=pl.BlockSpec((TM, d), lambda i: (i, 0)),
        out_shape=jax.ShapeDtypeStruct((n, d), F32),
        compiler_params=pltpu.CompilerParams(dimension_semantics=("arbitrary",),
                                             vmem_limit_bytes=VMEM_LIMIT),
        name="final_combine",
    )(x1, y4, y4, y4, y4, gate_nk, mod, post_g.reshape(1, d))


def _causal_conv(buf_ref, w_ref, width, halo, out_ref, bias_ref=None):
    tm, d = out_ref.shape
    for c0 in range(0, d, COL_CHUNK):
        cols = slice(c0, c0 + COL_CHUNK)
        for r0 in range(0, tm, ROW_CHUNK):
            acc = None
            for k in range(width):
                start = halo + r0 - (width - 1) + k
                term = w_ref[k:k + 1, cols] * buf_ref[start:start + ROW_CHUNK, cols]
                acc = term if acc is None else acc + term
            if bias_ref is not None:
                acc = acc + bias_ref[:, cols]
            out_ref[r0:r0 + ROW_CHUNK, cols] = acc


def _mix_kernel(*refs, steps_per_seq, has_comb):
    n_in = 8 if has_comb else 1
    x_in = refs[:n_in]
    (mod_ref, pre_g, post_g, w_in, sc_conv, sc_out, cf_conv, cf_conv_b, cf_ln_g, cf_ln_b, cf_out,
     cf_out_b, pool_w, pool_scale, w_o, moe_pre_g, w_router_t, b_router,
     x1_ref, h2_ref, eidx_ref, gate_ref, rank_ref, counts_ref,
     cx_buf, v_buf, u_buf, tmp_buf, carry_ref) = refs[n_in:]
    i = pl.program_id(0)
    t = i % steps_per_seq
    tm, d = x1_ref.shape

    @pl.when(t == 0)
    def _():
        cx_buf[0:SC_HALO, :] = jnp.zeros((SC_HALO, d), F32)
        v_buf[0:CF_HALO, :] = jnp.zeros((CF_HALO, d), F32)
        u_buf[0:POOL_HALO, :] = jnp.zeros((POOL_HALO, d), F32)

    @pl.when(i == 0)
    def _():
        carry_ref[...] = jnp.zeros(carry_ref.shape, F32)

    if has_comb:
        xp_ref, y0, y1, y2, y3, gate_prev, mod_prev, post_g_prev = x_in
        x = _combine(xp_ref, (y0, y1, y2, y3), gate_prev, mod_prev, post_g_prev)
    else:
        x = x_in[0][...]
    shift1, scale1, gate1 = mod_ref[0, 0:1, :], mod_ref[0, 1:2, :], mod_ref[0, 2:3, :]
    shift2, scale2 = mod_ref[0, 3:4, :], mod_ref[0, 4:5, :]

    hb = (_rms(x, pre_g[...]) * (1.0 + scale1) + shift1).astype(BF16)

    def proj(j):
        return jnp.dot(hb, w_in[:, j * d:(j + 1) * d], preferred_element_type=F32)

    cx_buf[SC_HALO:SC_HALO + tm, :] = proj(1) * proj(2)
    _causal_conv(cx_buf, sc_conv, SC_WIDTH, SC_HALO, tmp_buf)
    y_a = jnp.dot((proj(0) * tmp_buf[...]).astype(BF16), sc_out[...], preferred_element_type=F32)
    cx_buf[0:SC_HALO, :] = cx_buf[tm:tm + SC_HALO, :]
    merged = _sigmoid(proj(6)) * y_a

    v_buf[CF_HALO:CF_HALO + tm, :] = proj(3) * _sigmoid(proj(4))
    _causal_conv(v_buf, cf_conv, CF_WIDTH, CF_HALO, tmp_buf, cf_conv_b)
    v_buf[0:CF_HALO, :] = v_buf[tm:tm + CF_HALO, :]
    v = tmp_buf[...]
    mu = jnp.mean(v, axis=-1, keepdims=True)
    vc = v - mu
    var = jnp.mean(vc * vc, axis=-1, keepdims=True)
    vn = vc * lax.rsqrt(var + LN_EPS) * cf_ln_g[...] + cf_ln_b[...]
    vn = vn * _sigmoid(vn)
    y_b = jnp.dot(vn.astype(BF16), cf_out[...], preferred_element_type=F32) + cf_out_b[...]
    merged = merged + _sigmoid(proj(7)) * y_b

    u = proj(5)
    u_buf[POOL_HALO:POOL_HALO + tm, :] = u
    pos = (t * tm + 1 + lax.broadcasted_iota(jnp.int32, (tm, 1), 0)).astype(F32)
    group = d // len(POOL_WINDOWS)
    y_c_parts = []
    for g, w in enumerate(POOL_WINDOWS):
        cols = slice(g * group, (g + 1) * group)
        s = u[:, cols]
        for j in range(1, w):
            s = s + u_buf[POOL_HALO - j:POOL_HALO - j + tm, cols]
        p = s / jnp.minimum(pos, float(w)) - u[:, cols]
        y_c_parts.append(jnp.dot(p.astype(BF16), pool_w[g], preferred_element_type=F32))
    u_buf[0:POOL_HALO, :] = u_buf[tm:tm + POOL_HALO, :]
    y_c = jnp.concatenate(y_c_parts, axis=-1) * pool_scale[...]
    merged = merged + _sigmoid(proj(8)) * y_c

    y = jnp.dot(merged.astype(BF16), w_o[...], preferred_element_type=F32)
    x1 = x + gate1 * _rms(y, post_g[...])
    x1_ref[...] = x1

    h2 = _rms(x1, moe_pre_g[...]) * (1.0 + scale2) + shift2
    _store_row_tiled(h2_ref, 0, h2)
    logits = lax.dot_general(w_router_t[...], h2, (((1,), (1,)), ((), ())),
                             preferred_element_type=F32,
                             precision=lax.Precision.HIGHEST) + b_router[...]
    eio = lax.broadcasted_iota(jnp.int32, (N_EXPERTS, tm), 0)
    neg_inf = jnp.float32(-jnp.inf)
    vals, idxs = [], []
    for _ in range(TOP_K):
        m = jnp.max(logits, axis=0, keepdims=True)
        idx = jnp.min(jnp.where(logits == m, eio, N_EXPERTS), axis=0, keepdims=True)
        vals.append(m)
        idxs.append(idx)
        logits = jnp.where(eio == idx, neg_inf, logits)
    exps = [jnp.exp(v - vals[0]) for v in vals]
    denom = exps[0] + exps[1] + exps[2] + exps[3]
    onehot = jnp.zeros((N_EXPERTS, tm), F32)
    for idx in idxs:
        onehot = onehot + (eio == idx).astype(F32)
    tri = (lax.broadcasted_iota(jnp.int32, (tm, tm), 0)
           < lax.broadcasted_iota(jnp.int32, (tm, tm), 1)).astype(BF16)
    carry = carry_ref[:, 0:1]
    before = jnp.dot(onehot.astype(BF16), tri, preferred_element_type=F32) + carry
    for k in range(TOP_K):
        eidx_ref[k:k + 1, :] = idxs[k]
        gate_ref[k:k + 1, :] = exps[k] / denom
        rank_ref[k:k + 1, :] = jnp.sum(jnp.where(eio == idxs[k], before, 0.0), axis=0,
                                       keepdims=True).astype(jnp.int32)
    carry = carry + jnp.sum(onehot, axis=1, keepdims=True)
    carry_ref[...] = jnp.broadcast_to(carry, carry_ref.shape)
    counts_ref[...] = jnp.broadcast_to(carry, counts_ref.shape).astype(jnp.int32)


def _mix_call(x2d, comb, mod, p, seq_len):
    has_comb = comb is not None
    n, d = comb[0].shape if has_comb else x2d.shape
    steps_per_seq = seq_len // TM
    n_steps = n // TM
    row = lambda a: a.reshape(1, d)
    if has_comb:
        x1p, y4p, gate_p, mod_p, post_g_p = comb
        lead_args = (x1p, y4p, y4p, y4p, y4p, gate_p, mod_p, row(post_g_p))
        lead_specs = _combine_specs(n, d, steps_per_seq)
    else:
        lead_args = (x2d,)
        lead_specs = [pl.BlockSpec((TM, d), lambda i: (i, 0))]
    const_args = (
        row(p["pre_g"]), row(p["post_g"]), p["w_in"], p["sc_conv"], p["sc_out"], p["cf_conv"],
        row(p["cf_conv_b"]), row(p["cf_ln_g"]), row(p["cf_ln_b"]), p["cf_out"], row(p["cf_out_b"]),
        p["pool_w"], row(p["pool_scale"]), p["w_o"], row(p["moe_pre_g"]), p["w_router_t"],
        p["b_router"].reshape(N_EXPERTS, 1),
    )
    in_specs = (lead_specs + [pl.BlockSpec((1, N_MOD, d), lambda i: (i // steps_per_seq, 0, 0))]
                + [_const_spec(a.shape) for a in const_args])
    tok_spec = pl.BlockSpec((TOP_K, TM), lambda i: (0, i))
    out_shape = (
        jax.ShapeDtypeStruct((n, d), F32),
        jax.ShapeDtypeStruct((n * SUBLANES, LANES), F32),
        jax.ShapeDtypeStruct((TOP_K, n), jnp.int32),
        jax.ShapeDtypeStruct((TOP_K, n), F32),
        jax.ShapeDtypeStruct((TOP_K, n), jnp.int32),
        jax.ShapeDtypeStruct((N_EXPERTS, LANES), jnp.int32),
    )
    out_specs = (
        pl.BlockSpec((TM, d), lambda i: (i, 0)),
        pl.BlockSpec((TM * SUBLANES, LANES), lambda i: (i, 0)),
        tok_spec, tok_spec, tok_spec,
        pl.BlockSpec((N_EXPERTS, LANES), lambda i: (0, 0)),
    )
    return pl.pallas_call(
        functools.partial(_mix_kernel, steps_per_seq=steps_per_seq, has_comb=has_comb),
        grid=(n_steps,),
        in_specs=in_specs,
        out_specs=out_specs,
        out_shape=out_shape,
        scratch_shapes=[
            pltpu.VMEM((SC_HALO + TM, d), F32),
            pltpu.VMEM((CF_HALO + TM, d), F32),
            pltpu.VMEM((POOL_HALO + TM, d), F32),
            pltpu.VMEM((TM, d), F32),
            pltpu.VMEM((N_EXPERTS, LANES), F32),
        ],
        compiler_params=pltpu.CompilerParams(dimension_semantics=("arbitrary",),
                                             vmem_limit_bytes=VMEM_LIMIT),
        name="mixer_router",
    )(*lead_args, mod, *const_args)


def _route_kernel(pstart_ref, pvalid_ref, pend_ref, eidx_ref, rank_ref, inv_ref,
                  dest_vmem, dest_smem, sem, *, n_chunks, n_asg):
    c = pl.program_id(0)
    n_pad = inv_ref.shape[0]
    chunk = ROUTE_ROWS * LANES

    def to_smem(s):
        return pltpu.make_async_copy(dest_vmem.at[s], dest_smem.at[s], sem.at[s])

    @pl.when(c == 0)
    def _():
        def pad_slot(p, carry):
            inv_ref[p] = n_asg + (p & (2 * BM - 1))
            return carry

        def pad_expert(e, carry):
            lax.fori_loop(pvalid_ref[e], pend_ref[e], pad_slot, 0)
            return carry

        lax.fori_loop(0, N_EXPERTS, pad_expert, 0)
        lax.fori_loop(pend_ref[N_EXPERTS - 1], n_pad, pad_slot, 0)

    @pl.when(c < n_chunks)
    def _():
        eidx = eidx_ref[...]
        dest = rank_ref[...]
        for e in range(N_EXPERTS):
            dest = dest + jnp.where(eidx == e, pstart_ref[e], 0)
        dest_vmem[c % 2] = dest
        to_smem(c % 2).start()

    @pl.when(c >= 1)
    def _():
        s = (c - 1) % 2
        to_smem(s).wait()
        base = (c - 1) * chunk

        def row(r, carry):
            for col in range(LANES):
                inv_ref[dest_smem[s, r, col]] = base + r * LANES + col
            return carry

        lax.fori_loop(0, ROUTE_ROWS, row, 0)


def _routing_tables(eidx, rank, counts):
    n = eidx.shape[1]
    n_asg = n * TOP_K
    n_pad = n_asg + N_EXPERTS * BM
    nb_max = n_pad // BM
    chunk = ROUTE_ROWS * LANES
    n_chunks = n_asg // chunk
    padded = (counts + BM - 1) // BM * BM
    pend = jnp.cumsum(padded).astype(jnp.int32)
    pstart = pend - padded
    n_blocks = pend[-1:] // BM
    blk = jnp.arange(nb_max, dtype=jnp.int32)
    block_e = jnp.minimum(jnp.sum((pend[None, :] <= blk[:, None] * BM).astype(jnp.int32), axis=1),
                          N_EXPERTS - 1)
    last_e = jnp.sum(jnp.where(blk == n_blocks[0] - 1, block_e, 0))
    block_e = jnp.where(blk < n_blocks[0], block_e, last_e).astype(jnp.int32)
    smem = pl.BlockSpec(memory_space=pltpu.SMEM)
    tile = pl.BlockSpec((ROUTE_ROWS, LANES), lambda c: (jnp.minimum(c, n_chunks - 1), 0))
    inv = pl.pallas_call(
        functools.partial(_route_kernel, n_chunks=n_chunks, n_asg=n_asg),
        grid=(n_chunks + 1,),
        in_specs=[smem, smem, smem, tile, tile],
        out_specs=smem,
        out_shape=jax.ShapeDtypeStruct((n_pad,), jnp.int32),
        scratch_shapes=[
            pltpu.VMEM((2, ROUTE_ROWS, LANES), jnp.int32),
            pltpu.SMEM((2, ROUTE_ROWS, LANES), jnp.int32),
            pltpu.SemaphoreType.DMA((2,)),
        ],
        compiler_params=pltpu.CompilerParams(dimension_semantics=("arbitrary",)),
        name="route_tables",
    )(pstart, pstart + counts, pend, eidx.reshape(n_asg // LANES, LANES),
      rank.reshape(n_asg // LANES, LANES))
    return inv, block_e, n_blocks


def _expert_kernel(be_ref, nb_ref, inv_hbm, h2_hbm, wgu_ref, bgu_ref, wd_ref, bd_ref, y4_hbm,
                   idx_smem, xbuf, ybuf, wgu_bf, wd_bf, isem, gsem, ssem, zsem,
                   *, nb_max, n_tok):
    i = pl.program_id(0)
    nb = nb_ref[0]
    g_cur = i % 2
    s_cur = i % 3
    s_next = (i + 1) % 3
    s_prev = (i + 2) % 3
    rows = BM * SUBLANES
    d_ff = wd_bf.shape[0]
    n_asg = TOP_K * n_tok

    def idx_copy(blk, s):
        return pltpu.make_async_copy(inv_hbm.at[pl.ds(blk * BM, BM)], idx_smem.at[s], isem.at[s])

    def gather_copy(s_idx, g, r):
        tok = idx_smem[s_idx, r] & (n_tok - 1)
        return pltpu.make_async_copy(
            h2_hbm.at[pl.ds(pl.multiple_of(tok * SUBLANES, SUBLANES), SUBLANES)],
            xbuf.at[pl.ds(pl.multiple_of(g * rows + r * SUBLANES, SUBLANES), SUBLANES)],
            gsem.at[g])

    def scatter_copy(s, r):
        dst = idx_smem[s, r]
        return pltpu.make_async_copy(
            ybuf.at[pl.ds(pl.multiple_of(s * rows + r * SUBLANES, SUBLANES), SUBLANES)],
            y4_hbm.at[pl.ds(pl.multiple_of(dst * SUBLANES, SUBLANES), SUBLANES)],
            ssem.at[s])

    def wait_gather(g):
        pltpu.make_async_copy(h2_hbm.at[pl.ds(0, rows)], xbuf.at[pl.ds(g * rows, rows)],
                              gsem.at[g]).wait()

    def wait_scatter(s):
        pltpu.make_async_copy(ybuf.at[pl.ds(s * rows, rows)], y4_hbm.at[pl.ds(0, rows)],
                              ssem.at[s]).wait()

    @pl.when(i == 0)
    def _():
        ybuf[...] = jnp.zeros(ybuf.shape, F32)
        spare = pltpu.make_async_copy(
            ybuf.at[pl.ds(0, DUMP_BLOCKS * rows)],
            y4_hbm.at[pl.ds(n_asg * SUBLANES, DUMP_BLOCKS * rows)], zsem)
        spare.start()
        spare.wait()

        def fill(r, carry):
            idx_smem[2, r] = n_asg + 2 * BM + r
            return carry
        lax.fori_loop(0, BM, fill, 0)
        idx_copy(0, 0).start()
        idx_copy(0, 0).wait()

        def body(r, carry):
            gather_copy(0, 0, r).start()
            return carry
        lax.fori_loop(0, BM, body, 0)
        idx_copy(1, 1).start()

    prev_e = be_ref[jnp.maximum(i - 1, 0)]

    @pl.when((i < nb) & ((i == 0) | (be_ref[i] != prev_e)))
    def _():
        wgu_bf[...] = wgu_ref[0, 0].astype(BF16)
        wd_bf[...] = wd_ref[0, 0].astype(BF16)

    @pl.when((i < nb) & (i >= 2))
    def _():
        wait_scatter(s_cur)

    @pl.when(i < nb)
    def _():
        wait_gather(g_cur)
        idx_copy(i + 1, s_next).wait()
        n_chunk = d_ff // MXU_DEPTH
        halves = MXU_DEPTH // LANES
        per_gather = -(-BM // (n_chunk - 1))
        per_scatter = BM // (2 * n_chunk)
        act_base = DUMP_BLOCKS * rows
        bgu = bgu_ref[0, 0]

        def scatter_group(g):
            for r in range(g * per_scatter, (g + 1) * per_scatter):
                scatter_copy(s_prev, r).start()

        for c in range(n_chunk):
            xb = _load_row_tiled(xbuf, g_cur * rows, BM).astype(BF16)
            if c < n_chunk - 1:
                for r in range(c * per_gather, min((c + 1) * per_gather, BM)):
                    gather_copy(s_next, 1 - g_cur, r).start()
            gcols = slice(c * MXU_DEPTH, (c + 1) * MXU_DEPTH)
            ucols = slice(d_ff + c * MXU_DEPTH, d_ff + (c + 1) * MXU_DEPTH)
            gt = jnp.dot(xb, wgu_bf[:, gcols], preferred_element_type=F32) + bgu[:, gcols]
            up = jnp.dot(xb, wgu_bf[:, ucols], preferred_element_type=F32) + bgu[:, ucols]
            gt = jnp.minimum(gt, SWIGLU_LIMIT)
            up = jnp.clip(up, -SWIGLU_LIMIT, SWIGLU_LIMIT)
            act_c = (up + 1.0) * gt * _sigmoid(gt * SWIGLU_ALPHA)
            scatter_group(c)
            for h in range(halves):
                ybuf[pl.ds(act_base + (c * halves + h) * BM, BM), :] = act_c[:, h * LANES:(h + 1) * LANES]
        act = jnp.concatenate(
            [ybuf[pl.ds(act_base + j * BM, BM), :] for j in range(n_chunk * halves)],
            axis=-1).astype(BF16)
        for c in range(n_chunk):
            cols = slice(c * MXU_DEPTH, (c + 1) * MXU_DEPTH)
            yc = jnp.dot(act, wd_bf[:, cols], preferred_element_type=F32) + bd_ref[0, 0][:, cols]
            scatter_group(n_chunk + c)
            for h in range(halves):
                ybuf[pl.ds(s_cur * rows + c * halves + h, BM, stride=SUBLANES), :] = (
                    yc[:, h * LANES:(h + 1) * LANES])

        @pl.when(i + 2 < nb_max)
        def _():
            idx_copy(i + 2, s_prev).start()

    @pl.when(i == nb)
    def _():
        wait_gather(g_cur)

        @pl.when(i + 1 < nb_max)
        def _():
            idx_copy(i + 1, s_next).wait()

        @pl.when(i >= 2)
        def _():
            wait_scatter(s_cur)

        def body(r, carry):
            scatter_copy(s_prev, r).start()
            return carry
        lax.fori_loop(0, BM, body, 0)
        wait_scatter(s_next)
        wait_scatter(s_prev)


def _expert_call(layer, block_e, n_blocks, inv, h2, w_gu, b_gu, w_down, b_down):
    n_tok = h2.shape[0] // SUBLANES
    depth, n_exp, d, two_ff = w_gu.shape
    d_ff = two_ff // 2
    nb_max = inv.shape[0] // BM
    rows = BM * SUBLANES
    grid_spec = pltpu.PrefetchScalarGridSpec(
        num_scalar_prefetch=2,
        grid=(nb_max,),
        in_specs=[
            pl.BlockSpec(memory_space=pl.ANY),
            pl.BlockSpec(memory_space=pl.ANY),
            pl.BlockSpec((1, 1, d, two_ff), lambda i, be, nb: (layer, be[i], 0, 0)),
            pl.BlockSpec((1, 1, 1, two_ff), lambda i, be, nb: (layer, be[i], 0, 0)),
            pl.BlockSpec((1, 1, d_ff, d), lambda i, be, nb: (layer, be[i], 0, 0)),
            pl.BlockSpec((1, 1, 1, d), lambda i, be, nb: (layer, be[i], 0, 0)),
        ],
        out_specs=pl.BlockSpec(memory_space=pl.ANY),
        scratch_shapes=[
            pltpu.SMEM((3, BM), jnp.int32),
            pltpu.VMEM((2 * rows, LANES), F32),
            pltpu.VMEM(((DUMP_BLOCKS + 1) * rows, LANES), F32),
            pltpu.VMEM((d, two_ff), BF16),
            pltpu.VMEM((d_ff, d), BF16),
            pltpu.SemaphoreType.DMA((3,)),
            pltpu.SemaphoreType.DMA((2,)),
            pltpu.SemaphoreType.DMA((3,)),
            pltpu.SemaphoreType.DMA(()),
        ],
    )
    return pl.pallas_call(
        functools.partial(_expert_kernel, nb_max=nb_max, n_tok=n_tok),
        grid_spec=grid_spec,
        out_shape=jax.ShapeDtypeStruct(((TOP_K * n_tok + DUMP_BLOCKS * BM) * SUBLANES, LANES), F32),
        compiler_params=pltpu.CompilerParams(dimension_semantics=("arbitrary",),
                                             vmem_limit_bytes=VMEM_LIMIT),
        name="experts",
    )(block_e, n_blocks, inv, h2, w_gu, b_gu.reshape(depth, n_exp, 1, two_ff), w_down,
      b_down.reshape(depth, n_exp, 1, d))


def kernel(x, c, ada_w, ada_b, mix_pre_g, mix_post_g, w_in, sc_conv, sc_out, cf_conv, cf_conv_b, cf_ln_g, cf_ln_b, cf_out, cf_out_b, pool_w, pool_scale, w_o, moe_pre_g, moe_post_g, w_router, b_router, w_gu, b_gu, w_down, b_down):
    bsz, seq_len, d = x.shape
    depth = ada_w.shape[0]
    n_tok = bsz * seq_len
    assert d == SUBLANES * LANES and seq_len % TM == 0 and d % COL_CHUNK == 0 and TM % ROW_CHUNK == 0
    assert (n_tok * TOP_K) % (ROUTE_ROWS * LANES) == 0 and (n_tok * TOP_K) % BM == 0
    assert n_tok >= 2 * BM and n_tok & (n_tok - 1) == 0 and BM & (BM - 1) == 0
    mods = _ada_call(c, ada_w, ada_b)
    xs = x.reshape(n_tok, d)
    comb = None
    for l in range(depth):
        p = dict(
            pre_g=mix_pre_g[l], post_g=mix_post_g[l], w_in=w_in[l].astype(BF16), sc_conv=sc_conv[l],
            sc_out=sc_out[l].astype(BF16), cf_conv=cf_conv[l], cf_conv_b=cf_conv_b[l], cf_ln_g=cf_ln_g[l],
            cf_ln_b=cf_ln_b[l], cf_out=cf_out[l].astype(BF16), cf_out_b=cf_out_b[l],
            pool_w=pool_w[l].astype(BF16), pool_scale=pool_scale[l], w_o=w_o[l].astype(BF16),
            moe_pre_g=moe_pre_g[l], w_router_t=w_router[l].T, b_router=b_router[l],
        )
        x1, h2, eidx, gate, rank, counts = _mix_call(xs if comb is None else None, comb, mods[l], p, seq_len)
        inv, block_e, n_blocks = _routing_tables(eidx, rank, counts[:, 0])
        y4 = _expert_call(l, block_e, n_blocks, inv, h2, w_gu, b_gu, w_down, b_down)
        comb = (x1, y4, gate.T, mods[l], moe_post_g[l])
    return _final_call(comb, seq_len).reshape(bsz, seq_len, d)
```

```python
import functools

import jax
import jax.numpy as jnp
from jax import lax
from jax.experimental import pallas as pl
from jax.experimental.pallas import tpu as pltpu

F32 = jnp.float32
BF16 = jnp.bfloat16

N_EXPERTS = 32
TOP_K = 4
N_MOD = 6
SC_WIDTH = 3
CF_WIDTH = 31
POOL_WINDOWS = (2, 4, 8, 16)
SWIGLU_LIMIT = 7.0
SWIGLU_ALPHA = 1.702
NORM_EPS = 1e-6
LN_EPS = 1e-5

SUBLANES = 8
LANES = 128
MXU_DEPTH = 256
TM = 256
BM = 256
SC_HALO = 8
CF_HALO = 32
POOL_HALO = 16
ROW_CHUNK = 32
COL_CHUNK = 256
ROUTE_ROWS = 16
DUMP_BLOCKS = 3
VMEM_LIMIT = 60 * 1024 * 1024


def _rms(x, g):
    return x * lax.rsqrt(jnp.mean(x * x, axis=-1, keepdims=True) + NORM_EPS) * g


def _sigmoid(x):
    return 0.5 * jnp.tanh(0.5 * x) + 0.5


def _const_spec(shape):
    nd = len(shape)
    return pl.BlockSpec(shape, lambda *_: (0,) * nd, pipeline_mode=pl.Buffered(1))


def _load_row_tiled(ref, base, rows):
    return jnp.concatenate(
        [ref[pl.ds(base + s, rows, stride=SUBLANES), :] for s in range(SUBLANES)], axis=-1)


def _store_row_tiled(ref, base, value):
    rows = value.shape[0]
    for s in range(SUBLANES):
        ref[pl.ds(base + s, rows, stride=SUBLANES), :] = value[:, s * LANES:(s + 1) * LANES]


def _ada_kernel(c_ref, w_ref, b_ref, o_ref):
    c = c_ref[...]
    c_act = c * _sigmoid(c)
    o_ref[0, 0] = jnp.dot(c_act, w_ref[0], preferred_element_type=F32,
                          precision=lax.Precision.HIGHEST) + b_ref[0, 0]


def _ada_call(c, ada_w, ada_b):
    depth, d, _ = ada_w.shape
    bsz = c.shape[0]
    ada_b4 = ada_b.reshape(depth, N_MOD, 1, d)
    out = pl.pallas_call(
        _ada_kernel,
        grid=(depth, N_MOD),
        in_specs=[
            pl.BlockSpec((bsz, d), lambda l, j: (0, 0)),
            pl.BlockSpec((1, d, d), lambda l, j: (l, 0, j)),
            pl.BlockSpec((1, 1, 1, d), lambda l, j: (l, j, 0, 0)),
        ],
        out_specs=pl.BlockSpec((1, 1, bsz, d), lambda l, j: (l, j, 0, 0)),
        out_shape=jax.ShapeDtypeStruct((depth, N_MOD, bsz, d), F32),
        name="ada_mod",
    )(c, ada_w, ada_b4)
    return jnp.transpose(out, (0, 2, 1, 3))


def _combine(x1_ref, y4_refs, gate_ref, mod_ref, post_g):
    tm = x1_ref.shape[0]
    gate = gate_ref[...]
    y = None
    for k, y_ref in enumerate(y4_refs):
        term = gate[:, k:k + 1] * _load_row_tiled(y_ref, 0, tm)
        y = term if y is None else y + term
    return x1_ref[...] + mod_ref[0, 5:6, :] * _rms(y, post_g[...])


def _combine_specs(n, d, steps_per_seq):
    blocks_per_plane = n // TM
    specs = [pl.BlockSpec((TM, d), lambda i: (i, 0))]
    for k in range(TOP_K):
        specs.append(pl.BlockSpec((TM * SUBLANES, LANES),
                                  lambda i, k=k: (k * blocks_per_plane + i, 0)))
    specs += [
        pl.BlockSpec((TM, TOP_K), lambda i: (i, 0)),
        pl.BlockSpec((1, N_MOD, d), lambda i: (i // steps_per_seq, 0, 0)),
        pl.BlockSpec((1, d), lambda i: (0, 0)),
    ]
    return specs


def _final_kernel(x1_ref, y0, y1, y2, y3, gate_ref, mod_ref, post_g, out_ref):
    out_ref[...] = _combine(x1_ref, (y0, y1, y2, y3), gate_ref, mod_ref, post_g)


def _final_call(comb, seq_len):
    x1, y4, gate_nk, mod, post_g = comb
    n, d = x1.shape
    return pl.pallas_call(
        _final_kernel,
        grid=(n // TM,),
        in_specs=_combine_specs(n, d, seq_len // TM),
        out_specs=pl.BlockSpec((TM, d), lambda i: (i, 0)),
        out_shape=jax.ShapeDtypeStruct((n, d), F32),
        compiler_params=pltpu.CompilerParams(dimension_semantics=("arbitrary",),
                                             vmem_limit_bytes=VMEM_LIMIT),
        name="final_combine",
    )(x1, y4, y4, y4, y4, gate_nk, mod, post_g.reshape(1, d))


def _causal_conv(buf_ref, w_ref, width, halo, out_ref, shift_ref, bias_ref=None, before_chunk=()):
    tm, d = out_ref.shape
    n_rows = buf_ref.shape[0] - SUBLANES
    offsets = [halo - (width - 1) + k for k in range(width)]
    for ci, c0 in enumerate(range(0, d, COL_CHUNK)):
        if ci < len(before_chunk):
            before_chunk[ci]()
        cols = slice(c0, c0 + COL_CHUNK)
        for q in sorted({o % SUBLANES for o in offsets} - {0}):
            shift_ref[q - 1, 0:n_rows, :] = buf_ref[q:q + n_rows, cols]
        for r0 in range(0, tm, ROW_CHUNK):
            accs = [None] * (ROW_CHUNK // SUBLANES)
            for k, o in enumerate(offsets):
                q = o % SUBLANES
                w8 = w_ref[k, :, cols]
                for j in range(len(accs)):
                    start = o - q + r0 + j * SUBLANES
                    if q == 0:
                        rows = buf_ref[start:start + SUBLANES, cols]
                    else:
                        rows = shift_ref[q - 1, start:start + SUBLANES, :]
                    accs[j] = w8 * rows if accs[j] is None else accs[j] + w8 * rows
            for j, acc in enumerate(accs):
                if bias_ref is not None:
                    acc = acc + bias_ref[:, cols]
                out_ref[r0 + j * SUBLANES:r0 + (j + 1) * SUBLANES, cols] = acc


def _mix_kernel(*refs, steps_per_seq, has_comb):
    n_in = 8 if has_comb else 1
    x_in = refs[:n_in]
    (mod_ref, pre_g, post_g, w_in, sc_conv, sc_out, cf_conv, cf_conv_b, cf_ln_g, cf_ln_b, cf_out,
     cf_out_b, pool_w, pool_scale, w_o, moe_pre_g, w_router_t, b_router, band_cur, band_prev,
     x1_ref, h2_ref, eidx_ref, gate_ref, rank_ref, counts_ref,
     cx_buf, v_buf, u_buf, tmp_buf, tmp2_buf, shift_buf, carry_ref) = refs[n_in:]
    i = pl.program_id(0)
    t = i % steps_per_seq
    tm, d = x1_ref.shape

    @pl.when(t == 0)
    def _():
        cx_buf[0:SC_HALO, :] = jnp.zeros((SC_HALO, d), F32)
        v_buf[0:CF_HALO, :] = jnp.zeros((CF_HALO, d), F32)
        u_buf[...] = jnp.zeros((POOL_HALO, d), F32)

    @pl.when(i == 0)
    def _():
        carry_ref[...] = jnp.zeros(carry_ref.shape, F32)

    if has_comb:
        xp_ref, y0, y1, y2, y3, gate_prev, mod_prev, post_g_prev = x_in
        x = _combine(xp_ref, (y0, y1, y2, y3), gate_prev, mod_prev, post_g_prev)
    else:
        x = x_in[0][...]
    shift1, scale1, gate1 = mod_ref[0, 0:1, :], mod_ref[0, 1:2, :], mod_ref[0, 2:3, :]
    shift2, scale2 = mod_ref[0, 3:4, :], mod_ref[0, 4:5, :]

    hb = (_rms(x, pre_g[...]) * (1.0 + scale1) + shift1).astype(BF16)

    def proj(j):
        return jnp.dot(hb, w_in[:, j * d:(j + 1) * d], preferred_element_type=F32)

    v_buf[CF_HALO:CF_HALO + tm, :] = proj(3) * _sigmoid(proj(4))
    side = {}

    def pool_in():
        side["u"] = proj(5)

    def branch_a():
        cx_buf[SC_HALO:SC_HALO + tm, :] = proj(1) * proj(2)
        _causal_conv(cx_buf, sc_conv, SC_WIDTH, SC_HALO, tmp2_buf, shift_buf)
        cx_buf[0:SC_HALO, :] = cx_buf[tm:tm + SC_HALO, :]
        conv_a = (proj(0) * tmp2_buf[...]).astype(BF16)
        side["merged"] = _sigmoid(proj(6)) * jnp.dot(conv_a, sc_out[...], preferred_element_type=F32)

    def gate_b():
        side["gate_b"] = _sigmoid(proj(7))

    def gate_c():
        side["gate_c"] = _sigmoid(proj(8))

    _causal_conv(v_buf, cf_conv, CF_WIDTH, CF_HALO, tmp_buf, shift_buf, cf_conv_b,
                 before_chunk=(pool_in, branch_a, gate_b, gate_c))
    v_buf[0:CF_HALO, :] = v_buf[tm:tm + CF_HALO, :]
    v = tmp_buf[...]
    mu = jnp.mean(v, axis=-1, keepdims=True)
    vc = v - mu
    var = jnp.mean(vc * vc, axis=-1, keepdims=True)
    vn = vc * lax.rsqrt(var + LN_EPS) * cf_ln_g[...] + cf_ln_b[...]
    vn = vn * _sigmoid(vn)
    y_b = jnp.dot(vn.astype(BF16), cf_out[...], preferred_element_type=F32) + cf_out_b[...]
    merged = side["merged"] + side["gate_b"] * y_b

    u = side["u"]
    pos = (t * tm + 1 + lax.broadcasted_iota(jnp.int32, (tm, 1), 0)).astype(F32)
    group = d // len(POOL_WINDOWS)

    def split(a):
        hi = a.astype(BF16)
        return jnp.concatenate([hi, (a - hi.astype(F32)).astype(BF16)], axis=-1)

    y_c_parts = []
    for g, w in enumerate(POOL_WINDOWS):
        cols = slice(g * group, (g + 1) * group)
        s2 = jnp.dot(band_cur[g], split(u[:, cols]), preferred_element_type=F32)
        h2s = jnp.dot(band_prev[g], split(u_buf[:, cols]), preferred_element_type=F32)
        s = s2[:, :group] + s2[:, group:]
        s = jnp.concatenate([s[:POOL_HALO] + h2s[:, :group] + h2s[:, group:], s[POOL_HALO:]], axis=0)
        p = s * (1.0 / jnp.minimum(pos, float(w))) - u[:, cols]
        y_c_parts.append(jnp.dot(p.astype(BF16), pool_w[g], preferred_element_type=F32))
    u_buf[...] = u[tm - POOL_HALO:, :]
    y_c = jnp.concatenate(y_c_parts, axis=-1) * pool_scale[...]
    merged = merged + side["gate_c"] * y_c

    y = jnp.dot(merged.astype(BF16), w_o[...], preferred_element_type=F32)
    x1 = x + gate1 * _rms(y, post_g[...])
    x1_ref[...] = x1

    h2 = _rms(x1, moe_pre_g[...]) * (1.0 + scale2) + shift2
    _store_row_tiled(h2_ref, 0, h2)
    h2_hi = h2.astype(BF16)
    h2_split = jnp.concatenate([h2_hi, (h2 - h2_hi.astype(F32)).astype(BF16)], axis=0)
    cross = lax.dot_general(w_router_t[...], h2_split, (((1,), (1,)), ((), ())),
                            preferred_element_type=F32)
    logits = (cross[:N_EXPERTS, :tm] + cross[N_EXPERTS:, :tm] + cross[:N_EXPERTS, tm:]
              + b_router[...])
    eio = lax.broadcasted_iota(jnp.int32, (N_EXPERTS, tm), 0)
    neg_inf = jnp.float32(-jnp.inf)
    vals, idxs = [], []
    for _ in range(TOP_K):
        m = jnp.max(logits, axis=0, keepdims=True)
        idx = jnp.min(jnp.where(logits == m, eio, N_EXPERTS), axis=0, keepdims=True)
        vals.append(m)
        idxs.append(idx)
        logits = jnp.where(eio == idx, neg_inf, logits)
    exps = [jnp.exp(v - vals[0]) for v in vals]
    denom = exps[0] + exps[1] + exps[2] + exps[3]
    onehot = jnp.zeros((N_EXPERTS, tm), F32)
    for idx in idxs:
        onehot = onehot + (eio == idx).astype(F32)
    tri = (lax.broadcasted_iota(jnp.int32, (tm, tm), 0)
           < lax.broadcasted_iota(jnp.int32, (tm, tm), 1)).astype(BF16)
    carry = carry_ref[:, 0:1]
    before = jnp.dot(onehot.astype(BF16), tri, preferred_element_type=F32) + carry
    for k in range(TOP_K):
        eidx_ref[k:k + 1, :] = idxs[k]
        gate_ref[k:k + 1, :] = exps[k] / denom
        rank_ref[k:k + 1, :] = jnp.sum(jnp.where(eio == idxs[k], before, 0.0), axis=0,
                                       keepdims=True).astype(jnp.int32)
    carry = carry + jnp.sum(onehot, axis=1, keepdims=True)
    carry_ref[...] = jnp.broadcast_to(carry, carry_ref.shape)
    counts_ref[...] = jnp.broadcast_to(carry, counts_ref.shape).astype(jnp.int32)


def _split_bf16(a):
    hi = a.astype(BF16)
    return jnp.concatenate([hi, (a - hi.astype(F32)).astype(BF16)], axis=0)


def _pool_bands():
    t = jnp.arange(TM)[:, None]
    j = jnp.arange(TM)[None, :]
    th = jnp.arange(POOL_HALO)[:, None]
    jh = jnp.arange(POOL_HALO)[None, :] - POOL_HALO
    cur = jnp.stack([(j <= t) & (j > t - w) for w in POOL_WINDOWS]).astype(BF16)
    prev = jnp.stack([jh > th - w for w in POOL_WINDOWS]).astype(BF16)
    return cur, prev


def _mix_call(x2d, comb, mod, p, seq_len):
    has_comb = comb is not None
    n, d = comb[0].shape if has_comb else x2d.shape
    steps_per_seq = seq_len // TM
    n_steps = n // TM
    row = lambda a: a.reshape(1, d)
    taps = lambda a: jnp.broadcast_to(a[:, None, :], (a.shape[0], SUBLANES, d))
    if has_comb:
        x1p, y4p, gate_p, mod_p, post_g_p = comb
        lead_args = (x1p, y4p, y4p, y4p, y4p, gate_p, mod_p, row(post_g_p))
        lead_specs = _combine_specs(n, d, steps_per_seq)
    else:
        lead_args = (x2d,)
        lead_specs = [pl.BlockSpec((TM, d), lambda i: (i, 0))]
    const_args = (
        row(p["pre_g"]), row(p["post_g"]), p["w_in"], taps(p["sc_conv"]), p["sc_out"], taps(p["cf_conv"]),
        row(p["cf_conv_b"]), row(p["cf_ln_g"]), row(p["cf_ln_b"]), p["cf_out"], row(p["cf_out_b"]),
        p["pool_w"], row(p["pool_scale"]), p["w_o"], row(p["moe_pre_g"]), p["w_router_t"],
        p["b_router"].reshape(N_EXPERTS, 1), *_pool_bands(),
    )
    in_specs = (lead_specs + [pl.BlockSpec((1, N_MOD, d), lambda i: (i // steps_per_seq, 0, 0))]
                + [_const_spec(a.shape) for a in const_args])
    tok_spec = pl.BlockSpec((TOP_K, TM), lambda i: (0, i))
    out_shape = (
        jax.ShapeDtypeStruct((n, d), F32),
        jax.ShapeDtypeStruct((n * SUBLANES, LANES), F32),
        jax.ShapeDtypeStruct((TOP_K, n), jnp.int32),
        jax.ShapeDtypeStruct((TOP_K, n), F32),
        jax.ShapeDtypeStruct((TOP_K, n), jnp.int32),
        jax.ShapeDtypeStruct((N_EXPERTS, LANES), jnp.int32),
    )
    out_specs = (
        pl.BlockSpec((TM, d), lambda i: (i, 0)),
        pl.BlockSpec((TM * SUBLANES, LANES), lambda i: (i, 0)),
        tok_spec, tok_spec, tok_spec,
        pl.BlockSpec((N_EXPERTS, LANES), lambda i: (0, 0)),
    )
    return pl.pallas_call(
        functools.partial(_mix_kernel, steps_per_seq=steps_per_seq, has_comb=has_comb),
        grid=(n_steps,),
        in_specs=in_specs,
        out_specs=out_specs,
        out_shape=out_shape,
        scratch_shapes=[
            pltpu.VMEM((SC_HALO + TM, d), F32),
            pltpu.VMEM((CF_HALO + TM, d), F32),
            pltpu.VMEM((POOL_HALO, d), F32),
            pltpu.VMEM((TM, d), F32),
            pltpu.VMEM((TM, d), F32),
            pltpu.VMEM((SUBLANES - 1, CF_HALO + TM - SUBLANES, COL_CHUNK), F32),
            pltpu.VMEM((N_EXPERTS, LANES), F32),
        ],
        compiler_params=pltpu.CompilerParams(dimension_semantics=("arbitrary",),
                                             vmem_limit_bytes=VMEM_LIMIT),
        name="mixer_router",
    )(*lead_args, mod, *const_args)


def _route_kernel(pstart_ref, pvalid_ref, pend_ref, eidx_ref, rank_ref, inv_ref,
                  dest_vmem, dest_smem, sem, *, n_chunks, n_asg):
    c = pl.program_id(0)
    n_pad = inv_ref.shape[0]
    chunk = ROUTE_ROWS * LANES

    def to_smem(s):
        return pltpu.make_async_copy(dest_vmem.at[s], dest_smem.at[s], sem.at[s])

    @pl.when(c == 0)
    def _():
        def pad_slot(p, carry):
            inv_ref[p] = n_asg + (p & (2 * BM - 1))
            return carry

        def pad_expert(e, carry):
            lax.fori_loop(pvalid_ref[e], pend_ref[e], pad_slot, 0)
            return carry

        lax.fori_loop(0, N_EXPERTS, pad_expert, 0)
        lax.fori_loop(pend_ref[N_EXPERTS - 1], n_pad, pad_slot, 0)

    @pl.when(c < n_chunks)
    def _():
        eidx = eidx_ref[...]
        dest = rank_ref[...]
        for e in range(N_EXPERTS):
            dest = dest + jnp.where(eidx == e, pstart_ref[e], 0)
        dest_vmem[c % 2] = dest
        to_smem(c % 2).start()

    @pl.when(c >= 1)
    def _():
        s = (c - 1) % 2
        to_smem(s).wait()
        base = (c - 1) * chunk

        def row(r, carry):
            for col in range(LANES):
                inv_ref[dest_smem[s, r, col]] = base + r * LANES + col
            return carry

        lax.fori_loop(0, ROUTE_ROWS, row, 0)


def _routing_tables(eidx, rank, counts):
    n = eidx.shape[1]
    n_asg = n * TOP_K
    n_pad = n_asg + N_EXPERTS * BM
    nb_max = n_pad // BM
    chunk = ROUTE_ROWS * LANES
    n_chunks = n_asg // chunk
    padded = (counts + BM - 1) // BM * BM
    pend = jnp.cumsum(padded).astype(jnp.int32)
    pstart = pend - padded
    n_blocks = pend[-1:] // BM
    blk = jnp.arange(nb_max, dtype=jnp.int32)
    block_e = jnp.minimum(jnp.sum((pend[None, :] <= blk[:, None] * BM).astype(jnp.int32), axis=1),
                          N_EXPERTS - 1)
    last_e = jnp.sum(jnp.where(blk == n_blocks[0] - 1, block_e, 0))
    block_e = jnp.where(blk < n_blocks[0], block_e, last_e).astype(jnp.int32)
    smem = pl.BlockSpec(memory_space=pltpu.SMEM)
    tile = pl.BlockSpec((ROUTE_ROWS, LANES), lambda c: (jnp.minimum(c, n_chunks - 1), 0))
    inv = pl.pallas_call(
        functools.partial(_route_kernel, n_chunks=n_chunks, n_asg=n_asg),
        grid=(n_chunks + 1,),
        in_specs=[smem, smem, smem, tile, tile],
        out_specs=smem,
        out_shape=jax.ShapeDtypeStruct((n_pad,), jnp.int32),
        scratch_shapes=[
            pltpu.VMEM((2, ROUTE_ROWS, LANES), jnp.int32),
            pltpu.SMEM((2, ROUTE_ROWS, LANES), jnp.int32),
            pltpu.SemaphoreType.DMA((2,)),
        ],
        compiler_params=pltpu.CompilerParams(dimension_semantics=("arbitrary",)),
        name="route_tables",
    )(pstart, pstart + counts, pend, eidx.reshape(n_asg // LANES, LANES),
      rank.reshape(n_asg // LANES, LANES))
    return inv, block_e, n_blocks


def _expert_kernel(be_ref, nb_ref, inv_hbm, h2_hbm, wgu_ref, bgu_ref, wd_ref, bd_ref, y4_hbm,
                   idx_smem, xbuf, ybuf, wgu_bf, wd_bf, isem, gsem, ssem, zsem,
                   *, nb_max, n_tok):
    i = pl.program_id(0)
    nb = nb_ref[0]
    g_cur = i % 2
    s_cur = i % 3
    s_next = (i + 1) % 3
    s_prev = (i + 2) % 3
    q_next = (i + 1) % 4
    q_next2 = (i + 2) % 4
    q_prev = (i + 3) % 4
    rows = BM * SUBLANES
    d_ff = wd_bf.shape[0]
    n_asg = TOP_K * n_tok

    def idx_copy(blk, s):
        return pltpu.make_async_copy(inv_hbm.at[pl.ds(blk * BM, BM)], idx_smem.at[s], isem.at[s])

    def gather_copy(s_idx, g, r):
        tok = idx_smem[s_idx, r] & (n_tok - 1)
        return pltpu.make_async_copy(
            h2_hbm.at[pl.ds(pl.multiple_of(tok * SUBLANES, SUBLANES), SUBLANES)],
            xbuf.at[pl.ds(pl.multiple_of(g * rows + r * SUBLANES, SUBLANES), SUBLANES)],
            gsem.at[g])

    def scatter_copy(s_idx, s, r):
        dst = idx_smem[s_idx, r]
        return pltpu.make_async_copy(
            ybuf.at[pl.ds(pl.multiple_of(s * rows + r * SUBLANES, SUBLANES), SUBLANES)],
            y4_hbm.at[pl.ds(pl.multiple_of(dst * SUBLANES, SUBLANES), SUBLANES)],
            ssem.at[s])

    def wait_gather(g):
        pltpu.make_async_copy(h2_hbm.at[pl.ds(0, rows)], xbuf.at[pl.ds(g * rows, rows)],
                              gsem.at[g]).wait()

    def wait_scatter(s):
        pltpu.make_async_copy(ybuf.at[pl.ds(s * rows, rows)], y4_hbm.at[pl.ds(0, rows)],
                              ssem.at[s]).wait()

    @pl.when(i == 0)
    def _():
        ybuf[...] = jnp.zeros(ybuf.shape, F32)
        spare = pltpu.make_async_copy(
            ybuf.at[pl.ds(0, DUMP_BLOCKS * rows)],
            y4_hbm.at[pl.ds(n_asg * SUBLANES, DUMP_BLOCKS * rows)], zsem)
        spare.start()
        spare.wait()

        def fill(r, carry):
            idx_smem[3, r] = n_asg + 2 * BM + r
            return carry
        lax.fori_loop(0, BM, fill, 0)
        idx_copy(0, 0).start()
        idx_copy(0, 0).wait()

        def body(r, carry):
            gather_copy(0, 0, r).start()
            return carry
        lax.fori_loop(0, BM, body, 0)
        idx_copy(1, 1).start()

    prev_e = be_ref[jnp.maximum(i - 1, 0)]

    @pl.when((i < nb) & ((i == 0) | (be_ref[i] != prev_e)))
    def _():
        wgu_bf[...] = wgu_ref[0, 0].astype(BF16)
        wd_bf[...] = wd_ref[0, 0].astype(BF16)

    @pl.when((i < nb) & (i >= 2))
    def _():
        wait_scatter(s_cur)

    @pl.when((i < nb) & (i + 2 < nb_max))
    def _():
        idx_copy(i + 2, q_next2).start()

    @pl.when(i < nb)
    def _():
        wait_gather(g_cur)
        idx_copy(i + 1, q_next).wait()
        n_chunk = d_ff // MXU_DEPTH
        halves = MXU_DEPTH // LANES
        per_gather = -(-BM // (n_chunk - 1))
        per_scatter = BM // (2 * n_chunk)
        act_base = DUMP_BLOCKS * rows
        bgu = bgu_ref[0, 0]

        def scatter_group(g):
            for r in range(g * per_scatter, (g + 1) * per_scatter):
                scatter_copy(q_prev, s_prev, r).start()

        for c in range(n_chunk):
            xb = _load_row_tiled(xbuf, g_cur * rows, BM).astype(BF16)
            if c < n_chunk - 1:
                for r in range(c * per_gather, min((c + 1) * per_gather, BM)):
                    gather_copy(q_next, 1 - g_cur, r).start()
            gcols = slice(c * MXU_DEPTH, (c + 1) * MXU_DEPTH)
            ucols = slice(d_ff + c * MXU_DEPTH, d_ff + (c + 1) * MXU_DEPTH)
            gt = jnp.dot(xb, wgu_bf[:, gcols], preferred_element_type=F32) + bgu[:, gcols]
            up = jnp.dot(xb, wgu_bf[:, ucols], preferred_element_type=F32) + bgu[:, ucols]
            gt = jnp.minimum(gt, SWIGLU_LIMIT)
            up = jnp.clip(up, -SWIGLU_LIMIT, SWIGLU_LIMIT)
            act_c = (up + 1.0) * gt * _sigmoid(gt * SWIGLU_ALPHA)
            scatter_group(c)
            for h in range(halves):
                ybuf[pl.ds(act_base + (c * halves + h) * BM, BM), :] = act_c[:, h * LANES:(h + 1) * LANES]
        act = jnp.concatenate(
            [ybuf[pl.ds(act_base + j * BM, BM), :] for j in range(n_chunk * halves)],
            axis=-1).astype(BF16)
        for c in range(n_chunk):
            cols = slice(c * MXU_DEPTH, (c + 1) * MXU_DEPTH)
            yc = jnp.dot(act, wd_bf[:, cols], preferred_element_type=F32) + bd_ref[0, 0][:, cols]
            scatter_group(n_chunk + c)
            for h in range(halves):
                ybuf[pl.ds(s_cur * rows + c * halves + h, BM, stride=SUBLANES), :] = (
                    yc[:, h * LANES:(h + 1) * LANES])

    @pl.when(i == nb)
    def _():
        wait_gather(g_cur)

        @pl.when(i + 1 < nb_max)
        def _():
            idx_copy(i + 1, q_next).wait()

        @pl.when(i >= 2)
        def _():
            wait_scatter(s_cur)

        def body(r, carry):
            scatter_copy(q_prev, s_prev, r).start()
            return carry
        lax.fori_loop(0, BM, body, 0)
        wait_scatter(s_next)
        wait_scatter(s_prev)


def _expert_call(layer, block_e, n_blocks, inv, h2, w_gu, b_gu, w_down, b_down):
    n_tok = h2.shape[0] // SUBLANES
    depth, n_exp, d, two_ff = w_gu.shape
    d_ff = two_ff // 2
    nb_max = inv.shape[0] // BM
    rows = BM * SUBLANES
    grid_spec = pltpu.PrefetchScalarGridSpec(
        num_scalar_prefetch=2,
        grid=(nb_max,),
        in_specs=[
            pl.BlockSpec(memory_space=pl.ANY),
            pl.BlockSpec(memory_space=pl.ANY),
            pl.BlockSpec((1, 1, d, two_ff), lambda i, be, nb: (layer, be[i], 0, 0)),
            pl.BlockSpec((1, 1, 1, two_ff), lambda i, be, nb: (layer, be[i], 0, 0)),
            pl.BlockSpec((1, 1, d_ff, d), lambda i, be, nb: (layer, be[i], 0, 0)),
            pl.BlockSpec((1, 1, 1, d), lambda i, be, nb: (layer, be[i], 0, 0)),
        ],
        out_specs=pl.BlockSpec(memory_space=pl.ANY),
        scratch_shapes=[
            pltpu.SMEM((4, BM), jnp.int32),
            pltpu.VMEM((2 * rows, LANES), F32),
            pltpu.VMEM(((DUMP_BLOCKS + 1) * rows, LANES), F32),
            pltpu.VMEM((d, two_ff), BF16),
            pltpu.VMEM((d_ff, d), BF16),
            pltpu.SemaphoreType.DMA((4,)),
            pltpu.SemaphoreType.DMA((2,)),
            pltpu.SemaphoreType.DMA((3,)),
            pltpu.SemaphoreType.DMA(()),
        ],
    )
    return pl.pallas_call(
        functools.partial(_expert_kernel, nb_max=nb_max, n_tok=n_tok),
        grid_spec=grid_spec,
        out_shape=jax.ShapeDtypeStruct(((TOP_K * n_tok + DUMP_BLOCKS * BM) * SUBLANES, LANES), F32),
        compiler_params=pltpu.CompilerParams(dimension_semantics=("arbitrary",),
                                             vmem_limit_bytes=VMEM_LIMIT),
        name="experts",
    )(block_e, n_blocks, inv, h2, w_gu, b_gu.reshape(depth, n_exp, 1, two_ff), w_down,
      b_down.reshape(depth, n_exp, 1, d))


def kernel(x, c, ada_w, ada_b, mix_pre_g, mix_post_g, w_in, sc_conv, sc_out, cf_conv, cf_conv_b, cf_ln_g, cf_ln_b, cf_out, cf_out_b, pool_w, pool_scale, w_o, moe_pre_g, moe_post_g, w_router, b_router, w_gu, b_gu, w_down, b_down):
    bsz, seq_len, d = x.shape
    depth = ada_w.shape[0]
    n_tok = bsz * seq_len
    assert d == SUBLANES * LANES and seq_len % TM == 0 and d % COL_CHUNK == 0 and TM % ROW_CHUNK == 0
    assert (n_tok * TOP_K) % (ROUTE_ROWS * LANES) == 0 and (n_tok * TOP_K) % BM == 0
    assert n_tok >= 2 * BM and n_tok & (n_tok - 1) == 0 and BM & (BM - 1) == 0
    mods = _ada_call(c, ada_w, ada_b)
    xs = x.reshape(n_tok, d)
    comb = None
    for l in range(depth):
        p = dict(
            pre_g=mix_pre_g[l], post_g=mix_post_g[l], w_in=w_in[l].astype(BF16), sc_conv=sc_conv[l],
            sc_out=sc_out[l].astype(BF16), cf_conv=cf_conv[l], cf_conv_b=cf_conv_b[l], cf_ln_g=cf_ln_g[l],
            cf_ln_b=cf_ln_b[l], cf_out=cf_out[l].astype(BF16), cf_out_b=cf_out_b[l],
            pool_w=pool_w[l].astype(BF16), pool_scale=pool_scale[l], w_o=w_o[l].astype(BF16),
            moe_pre_g=moe_pre_g[l], w_router_t=_split_bf16(w_router[l].T), b_router=b_router[l],
        )
        x1, h2, eidx, gate, rank, counts = _mix_call(xs if comb is None else None, comb, mods[l], p, seq_len)
        inv, block_e, n_blocks = _routing_tables(eidx, rank, counts[:, 0])
        y4 = _expert_call(l, block_e, n_blocks, inv, h2, w_gu, b_gu, w_down, b_down)
        comb = (x1, y4, gate.T, mods[l], moe_post_g[l])
    return _final_call(comb, seq_len).reshape(bsz, seq_len, d)
```

```python
import functools

import jax
import jax.numpy as jnp
from jax import lax
from jax.experimental import pallas as pl
from jax.experimental.pallas import tpu as pltpu

F32 = jnp.float32
BF16 = jnp.bfloat16

N_EXPERTS = 32
TOP_K = 4
N_MOD = 6
SC_WIDTH = 3
CF_WIDTH = 31
POOL_WINDOWS = (2, 4, 8, 16)
SWIGLU_LIMIT = 7.0
SWIGLU_ALPHA = 1.702
NORM_EPS = 1e-6
LN_EPS = 1e-5

SUBLANES = 8
LANES = 128
MXU_DEPTH = 256
TM = 256
BM = 256
SC_HALO = 8
CF_HALO = 32
POOL_HALO = 16
ROW_CHUNK = 32
COL_CHUNK = 256
ROUTE_ROWS = 16
DUMP_BLOCKS = 3
VMEM_LIMIT = 60 * 1024 * 1024


def _rms(x, g):
    return x * lax.rsqrt(jnp.mean(x * x, axis=-1, keepdims=True) + NORM_EPS) * g


def _sigmoid(x):
    return 0.5 * jnp.tanh(0.5 * x) + 0.5


def _const_spec(shape):
    nd = len(shape)
    return pl.BlockSpec(shape, lambda *_: (0,) * nd, pipeline_mode=pl.Buffered(1))


def _load_row_tiled(ref, base, rows):
    return jnp.concatenate(
        [ref[pl.ds(base + s, rows, stride=SUBLANES), :] for s in range(SUBLANES)], axis=-1)


def _store_row_tiled(ref, base, value):
    rows = value.shape[0]
    for s in range(SUBLANES):
        ref[pl.ds(base + s, rows, stride=SUBLANES), :] = value[:, s * LANES:(s + 1) * LANES]


def _ada_kernel(c_ref, w_ref, b_ref, o_ref):
    c = c_ref[...]
    c_act = c * _sigmoid(c)
    o_ref[0, 0] = jnp.dot(c_act, w_ref[0], preferred_element_type=F32,
                          precision=lax.Precision.HIGHEST) + b_ref[0, 0]


def _ada_call(c, ada_w, ada_b):
    depth, d, _ = ada_w.shape
    bsz = c.shape[0]
    ada_b4 = ada_b.reshape(depth, N_MOD, 1, d)
    out = pl.pallas_call(
        _ada_kernel,
        grid=(depth, N_MOD),
        in_specs=[
            pl.BlockSpec((bsz, d), lambda l, j: (0, 0)),
            pl.BlockSpec((1, d, d), lambda l, j: (l, 0, j)),
            pl.BlockSpec((1, 1, 1, d), lambda l, j: (l, j, 0, 0)),
        ],
        out_specs=pl.BlockSpec((1, 1, bsz, d), lambda l, j: (l, j, 0, 0)),
        out_shape=jax.ShapeDtypeStruct((depth, N_MOD, bsz, d), F32),
        name="ada_mod",
    )(c, ada_w, ada_b4)
    return jnp.transpose(out, (0, 2, 1, 3))


def _combine(x1_ref, y4_refs, gate_ref, mod_ref, post_g):
    tm = x1_ref.shape[0]
    gate = gate_ref[...]
    y = None
    for k, y_ref in enumerate(y4_refs):
        term = gate[:, k:k + 1] * _load_row_tiled(y_ref, 0, tm)
        y = term if y is None else y + term
    return x1_ref[...] + mod_ref[0, 5:6, :] * _rms(y, post_g[...])


def _combine_specs(n, d, steps_per_seq):
    blocks_per_plane = n // TM
    specs = [pl.BlockSpec((TM, d), lambda i: (i, 0))]
    for k in range(TOP_K):
        specs.append(pl.BlockSpec((TM * SUBLANES, LANES),
                                  lambda i, k=k: (k * blocks_per_plane + i, 0)))
    specs += [
        pl.BlockSpec((TM, TOP_K), lambda i: (i, 0)),
        pl.BlockSpec((1, N_MOD, d), lambda i: (i // steps_per_seq, 0, 0)),
        pl.BlockSpec((1, d), lambda i: (0, 0)),
    ]
    return specs


def _final_kernel(x1_ref, y0, y1, y2, y3, gate_ref, mod_ref, post_g, out_ref):
    out_ref[...] = _combine(x1_ref, (y0, y1, y2, y3), gate_ref, mod_ref, post_g)


def _final_call(comb, seq_len):
    x1, y4, gate_nk, mod, post_g = comb
    n, d = x1.shape
    return pl.pallas_call(
        _final_kernel,
        grid=(n // TM,),
        in_specs=_combine_specs(n, d, seq_len // TM),
        out_specs=pl.BlockSpec((TM, d), lambda i: (i, 0)),
        out_shape=jax.ShapeDtypeStruct((n, d), F32),
        compiler_params=pltpu.CompilerParams(dimension_semantics=("arbitrary",),
                                             vmem_limit_bytes=VMEM_LIMIT),
        name="final_combine",
    )(x1, y4, y4, y4, y4, gate_nk, mod, post_g.reshape(1, d))


def _causal_conv(buf_ref, w_ref, width, halo, out_ref, shift_ref, bias_ref=None, before_chunk=()):
    tm, d = out_ref.shape
    n_rows = buf_ref.shape[0] - SUBLANES
    offsets = [halo - (width - 1) + k for k in range(width)]
    for ci, c0 in enumerate(range(0, d, COL_CHUNK)):
        if ci < len(before_chunk):
            before_chunk[ci]()
        cols = slice(c0, c0 + COL_CHUNK)
        for q in sorted({o % SUBLANES for o in offsets} - {0}):
            shift_ref[q - 1, 0:n_rows, :] = buf_ref[q:q + n_rows, cols]
        for r0 in range(0, tm, ROW_CHUNK):
            accs = [None] * (ROW_CHUNK // SUBLANES)
            for k, o in enumerate(offsets):
                q = o % SUBLANES
                w8 = w_ref[k, :, cols]
                for j in range(len(accs)):
                    start = o - q + r0 + j * SUBLANES
                    if q == 0:
                        rows = buf_ref[start:start + SUBLANES, cols]
                    else:
                        rows = shift_ref[q - 1, start:start + SUBLANES, :]
                    accs[j] = w8 * rows if accs[j] is None else accs[j] + w8 * rows
            for j, acc in enumerate(accs):
                if bias_ref is not None:
                    acc = acc + bias_ref[:, cols]
                out_ref[r0 + j * SUBLANES:r0 + (j + 1) * SUBLANES, cols] = acc


def _mix_kernel(*refs, steps_per_seq, has_comb):
    n_in = 8 if has_comb else 1
    x_in = refs[:n_in]
    (mod_ref, pre_g, post_g, w_in, sc_conv, sc_out, cf_conv, cf_conv_b, cf_ln_g, cf_ln_b, cf_out,
     cf_out_b, pool_w, pool_scale, w_o, moe_pre_g, w_router_t, b_router, band_cur, band_prev,
     x1_ref, h2_ref, eidx_ref, gate_ref, rank_ref, counts_ref,
     cx_buf, v_buf, u_buf, tmp_buf, tmp2_buf, shift_buf, carry_ref) = refs[n_in:]
    i = pl.program_id(0)
    t = i % steps_per_seq
    tm, d = x1_ref.shape

    @pl.when(t == 0)
    def _():
        cx_buf[0:SC_HALO, :] = jnp.zeros((SC_HALO, d), F32)
        v_buf[0:CF_HALO, :] = jnp.zeros((CF_HALO, d), F32)
        u_buf[...] = jnp.zeros((POOL_HALO, d), F32)

    @pl.when(i == 0)
    def _():
        carry_ref[...] = jnp.zeros(carry_ref.shape, F32)

    if has_comb:
        xp_ref, y0, y1, y2, y3, gate_prev, mod_prev, post_g_prev = x_in
        x = _combine(xp_ref, (y0, y1, y2, y3), gate_prev, mod_prev, post_g_prev)
    else:
        x = x_in[0][...]
    shift1, scale1, gate1 = mod_ref[0, 0:1, :], mod_ref[0, 1:2, :], mod_ref[0, 2:3, :]
    shift2, scale2 = mod_ref[0, 3:4, :], mod_ref[0, 4:5, :]

    hb = (_rms(x, pre_g[...]) * (1.0 + scale1) + shift1).astype(BF16)

    def proj(j):
        return jnp.dot(hb, w_in[:, j * d:(j + 1) * d], preferred_element_type=F32)

    v_buf[CF_HALO:CF_HALO + tm, :] = proj(3) * _sigmoid(proj(4))
    side = {}

    def pool_in():
        side["u"] = proj(5)

    def branch_a():
        cx_buf[SC_HALO:SC_HALO + tm, :] = proj(1) * proj(2)
        _causal_conv(cx_buf, sc_conv, SC_WIDTH, SC_HALO, tmp2_buf, shift_buf)
        cx_buf[0:SC_HALO, :] = cx_buf[tm:tm + SC_HALO, :]
        conv_a = (proj(0) * tmp2_buf[...]).astype(BF16)
        side["merged"] = _sigmoid(proj(6)) * jnp.dot(conv_a, sc_out[...], preferred_element_type=F32)

    def gate_b():
        side["gate_b"] = _sigmoid(proj(7))

    def gate_c():
        side["gate_c"] = _sigmoid(proj(8))

    _causal_conv(v_buf, cf_conv, CF_WIDTH, CF_HALO, tmp_buf, shift_buf, cf_conv_b,
                 before_chunk=(pool_in, branch_a, gate_b, gate_c))
    v_buf[0:CF_HALO, :] = v_buf[tm:tm + CF_HALO, :]
    v = tmp_buf[...]
    mu = jnp.mean(v, axis=-1, keepdims=True)
    vc = v - mu
    var = jnp.mean(vc * vc, axis=-1, keepdims=True)
    vn = vc * lax.rsqrt(var + LN_EPS) * cf_ln_g[...] + cf_ln_b[...]
    vn = vn * _sigmoid(vn)
    y_b = jnp.dot(vn.astype(BF16), cf_out[...], preferred_element_type=F32) + cf_out_b[...]
    merged = side["merged"] + side["gate_b"] * y_b

    u = side["u"]
    pos = (t * tm + 1 + lax.broadcasted_iota(jnp.int32, (tm, 1), 0)).astype(F32)
    group = d // len(POOL_WINDOWS)

    def split(a):
        hi = a.astype(BF16)
        return jnp.concatenate([hi, (a - hi.astype(F32)).astype(BF16)], axis=-1)

    y_c_parts = []
    for g, w in enumerate(POOL_WINDOWS):
        cols = slice(g * group, (g + 1) * group)
        s2 = jnp.dot(band_cur[g], split(u[:, cols]), preferred_element_type=F32)
        h2s = jnp.dot(band_prev[g], split(u_buf[:, cols]), preferred_element_type=F32)
        s = s2[:, :group] + s2[:, group:]
        s = jnp.concatenate([s[:POOL_HALO] + h2s[:, :group] + h2s[:, group:], s[POOL_HALO:]], axis=0)
        p = s * (1.0 / jnp.minimum(pos, float(w))) - u[:, cols]
        y_c_parts.append(jnp.dot(p.astype(BF16), pool_w[g], preferred_element_type=F32))
    u_buf[...] = u[tm - POOL_HALO:, :]
    y_c = jnp.concatenate(y_c_parts, axis=-1) * pool_scale[...]
    merged = merged + side["gate_c"] * y_c

    y = jnp.dot(merged.astype(BF16), w_o[...], preferred_element_type=F32)
    x1 = x + gate1 * _rms(y, post_g[...])
    x1_ref[...] = x1

    h2 = _rms(x1, moe_pre_g[...]) * (1.0 + scale2) + shift2
    _store_row_tiled(h2_ref, 0, h2)
    h2_hi = h2.astype(BF16)
    h2_split = jnp.concatenate([h2_hi, (h2 - h2_hi.astype(F32)).astype(BF16)], axis=0)
    cross = lax.dot_general(w_router_t[...], h2_split, (((1,), (1,)), ((), ())),
                            preferred_element_type=F32)
    logits = (cross[:N_EXPERTS, :tm] + cross[N_EXPERTS:, :tm] + cross[:N_EXPERTS, tm:]
              + b_router[...])
    eio = lax.broadcasted_iota(jnp.int32, (N_EXPERTS, tm), 0)
    neg_inf = jnp.float32(-jnp.inf)
    vals, idxs = [], []
    for _ in range(TOP_K):
        m = jnp.max(logits, axis=0, keepdims=True)
        idx = jnp.min(jnp.where(logits == m, eio, N_EXPERTS), axis=0, keepdims=True)
        vals.append(m)
        idxs.append(idx)
        logits = jnp.where(eio == idx, neg_inf, logits)
    exps = [jnp.exp(v - vals[0]) for v in vals]
    denom = exps[0] + exps[1] + exps[2] + exps[3]
    onehot = jnp.zeros((N_EXPERTS, tm), F32)
    for idx in idxs:
        onehot = onehot + (eio == idx).astype(F32)
    tri = (lax.broadcasted_iota(jnp.int32, (tm, tm), 0)
           < lax.broadcasted_iota(jnp.int32, (tm, tm), 1)).astype(BF16)
    carry = carry_ref[:, 0:1]
    before = jnp.dot(onehot.astype(BF16), tri, preferred_element_type=F32) + carry
    for k in range(TOP_K):
        eidx_ref[k:k + 1, :] = idxs[k]
        gate_ref[k:k + 1, :] = exps[k] / denom
        rank_ref[k:k + 1, :] = jnp.sum(jnp.where(eio == idxs[k], before, 0.0), axis=0,
                                       keepdims=True).astype(jnp.int32)
    carry = carry + jnp.sum(onehot, axis=1, keepdims=True)
    carry_ref[...] = jnp.broadcast_to(carry, carry_ref.shape)
    counts_ref[...] = jnp.broadcast_to(carry, counts_ref.shape).astype(jnp.int32)


def _split_bf16(a):
    hi = a.astype(BF16)
    return jnp.concatenate([hi, (a - hi.astype(F32)).astype(BF16)], axis=0)


def _pool_bands():
    t = jnp.arange(TM)[:, None]
    j = jnp.arange(TM)[None, :]
    th = jnp.arange(POOL_HALO)[:, None]
    jh = jnp.arange(POOL_HALO)[None, :] - POOL_HALO
    cur = jnp.stack([(j <= t) & (j > t - w) for w in POOL_WINDOWS]).astype(BF16)
    prev = jnp.stack([jh > th - w for w in POOL_WINDOWS]).astype(BF16)
    return cur, prev


def _mix_call(x2d, comb, mod, p, seq_len):
    has_comb = comb is not None
    n, d = comb[0].shape if has_comb else x2d.shape
    steps_per_seq = seq_len // TM
    n_steps = n // TM
    row = lambda a: a.reshape(1, d)
    taps = lambda a: jnp.broadcast_to(a[:, None, :], (a.shape[0], SUBLANES, d))
    if has_comb:
        x1p, y4p, gate_p, mod_p, post_g_p = comb
        lead_args = (x1p, y4p, y4p, y4p, y4p, gate_p, mod_p, row(post_g_p))
        lead_specs = _combine_specs(n, d, steps_per_seq)
    else:
        lead_args = (x2d,)
        lead_specs = [pl.BlockSpec((TM, d), lambda i: (i, 0))]
    const_args = (
        row(p["pre_g"]), row(p["post_g"]), p["w_in"], taps(p["sc_conv"]), p["sc_out"], taps(p["cf_conv"]),
        row(p["cf_conv_b"]), row(p["cf_ln_g"]), row(p["cf_ln_b"]), p["cf_out"], row(p["cf_out_b"]),
        p["pool_w"], row(p["pool_scale"]), p["w_o"], row(p["moe_pre_g"]), p["w_router_t"],
        p["b_router"].reshape(N_EXPERTS, 1), *_pool_bands(),
    )
    in_specs = (lead_specs + [pl.BlockSpec((1, N_MOD, d), lambda i: (i // steps_per_seq, 0, 0))]
                + [_const_spec(a.shape) for a in const_args])
    tok_spec = pl.BlockSpec((TOP_K, TM), lambda i: (0, i))
    out_shape = (
        jax.ShapeDtypeStruct((n, d), F32),
        jax.ShapeDtypeStruct((n * SUBLANES, LANES), F32),
        jax.ShapeDtypeStruct((TOP_K, n), jnp.int32),
        jax.ShapeDtypeStruct((TOP_K, n), F32),
        jax.ShapeDtypeStruct((TOP_K, n), jnp.int32),
        jax.ShapeDtypeStruct((N_EXPERTS, LANES), jnp.int32),
    )
    out_specs = (
        pl.BlockSpec((TM, d), lambda i: (i, 0)),
        pl.BlockSpec((TM * SUBLANES, LANES), lambda i: (i, 0)),
        tok_spec, tok_spec, tok_spec,
        pl.BlockSpec((N_EXPERTS, LANES), lambda i: (0, 0)),
    )
    return pl.pallas_call(
        functools.partial(_mix_kernel, steps_per_seq=steps_per_seq, has_comb=has_comb),
        grid=(n_steps,),
        in_specs=in_specs,
        out_specs=out_specs,
        out_shape=out_shape,
        scratch_shapes=[
            pltpu.VMEM((SC_HALO + TM, d), F32),
            pltpu.VMEM((CF_HALO + TM, d), F32),
            pltpu.VMEM((POOL_HALO, d), F32),
            pltpu.VMEM((TM, d), F32),
            pltpu.VMEM((TM, d), F32),
            pltpu.VMEM((SUBLANES - 1, CF_HALO + TM - SUBLANES, COL_CHUNK), F32),
            pltpu.VMEM((N_EXPERTS, LANES), F32),
        ],
        compiler_params=pltpu.CompilerParams(dimension_semantics=("arbitrary",),
                                             vmem_limit_bytes=VMEM_LIMIT),
        name="mixer_router",
    )(*lead_args, mod, *const_args)


def _route_kernel(pstart_ref, pvalid_ref, pend_ref, eidx_ref, rank_ref, inv_ref,
                  dest_vmem, dest_smem, sem, *, n_chunks, n_asg):
    c = pl.program_id(0)
    n_pad = inv_ref.shape[0]
    chunk = ROUTE_ROWS * LANES

    def to_smem(s):
        return pltpu.make_async_copy(dest_vmem.at[s], dest_smem.at[s], sem.at[s])

    @pl.when(c == 0)
    def _():
        def pad_slot(p, carry):
            inv_ref[p] = n_asg + (p & (2 * BM - 1))
            return carry

        def pad_expert(e, carry):
            lax.fori_loop(pvalid_ref[e], pend_ref[e], pad_slot, 0)
            return carry

        lax.fori_loop(0, N_EXPERTS, pad_expert, 0)
        lax.fori_loop(pend_ref[N_EXPERTS - 1], n_pad, pad_slot, 0)

    @pl.when(c < n_chunks)
    def _():
        eidx = eidx_ref[...]
        dest = rank_ref[...]
        for e in range(N_EXPERTS):
            dest = dest + jnp.where(eidx == e, pstart_ref[e], 0)
        dest_vmem[c % 2] = dest
        to_smem(c % 2).start()

    @pl.when(c >= 1)
    def _():
        s = (c - 1) % 2
        to_smem(s).wait()
        base = (c - 1) * chunk

        def row(r, carry):
            for col in range(LANES):
                inv_ref[dest_smem[s, r, col]] = base + r * LANES + col
            return carry

        lax.fori_loop(0, ROUTE_ROWS, row, 0)


def _routing_tables(eidx, rank, counts):
    n = eidx.shape[1]
    n_asg = n * TOP_K
    n_pad = n_asg + N_EXPERTS * BM
    nb_max = n_pad // BM
    chunk = ROUTE_ROWS * LANES
    n_chunks = n_asg // chunk
    padded = (counts + BM - 1) // BM * BM
    pend = jnp.cumsum(padded).astype(jnp.int32)
    pstart = pend - padded
    n_blocks = pend[-1:] // BM
    blk = jnp.arange(nb_max, dtype=jnp.int32)
    block_e = jnp.minimum(jnp.sum((pend[None, :] <= blk[:, None] * BM).astype(jnp.int32), axis=1),
                          N_EXPERTS - 1)
    last_e = jnp.sum(jnp.where(blk == n_blocks[0] - 1, block_e, 0))
    block_e = jnp.where(blk < n_blocks[0], block_e, last_e).astype(jnp.int32)
    smem = pl.BlockSpec(memory_space=pltpu.SMEM)
    tile = pl.BlockSpec((ROUTE_ROWS, LANES), lambda c: (jnp.minimum(c, n_chunks - 1), 0))
    inv = pl.pallas_call(
        functools.partial(_route_kernel, n_chunks=n_chunks, n_asg=n_asg),
        grid=(n_chunks + 1,),
        in_specs=[smem, smem, smem, tile, tile],
        out_specs=smem,
        out_shape=jax.ShapeDtypeStruct((n_pad,), jnp.int32),
        scratch_shapes=[
            pltpu.VMEM((2, ROUTE_ROWS, LANES), jnp.int32),
            pltpu.SMEM((2, ROUTE_ROWS, LANES), jnp.int32),
            pltpu.SemaphoreType.DMA((2,)),
        ],
        compiler_params=pltpu.CompilerParams(dimension_semantics=("arbitrary",)),
        name="route_tables",
    )(pstart, pstart + counts, pend, eidx.reshape(n_asg // LANES, LANES),
      rank.reshape(n_asg // LANES, LANES))
    return inv, block_e, n_blocks


def _expert_kernel(be_ref, nb_ref, inv_hbm, h2_hbm, wgu_ref, bgu_ref, wd_ref, bd_ref, y4_hbm,
                   idx_smem, xbuf, ybuf, wgu_bf, wd_bf, isem, gsem, ssem, zsem,
                   *, nb_max, n_tok):
    i = pl.program_id(0)
    nb = nb_ref[0]
    g_cur = i % 2
    s_cur = i % 3
    s_next = (i + 1) % 3
    s_prev = (i + 2) % 3
    q_next = (i + 1) % 4
    q_next2 = (i + 2) % 4
    q_prev = (i + 3) % 4
    rows = BM * SUBLANES
    d_ff = wd_bf.shape[0]
    n_asg = TOP_K * n_tok

    def idx_copy(blk, s):
        return pltpu.make_async_copy(inv_hbm.at[pl.ds(blk * BM, BM)], idx_smem.at[s], isem.at[s])

    def gather_copy(s_idx, g, r):
        tok = idx_smem[s_idx, r] & (n_tok - 1)
        return pltpu.make_async_copy(
            h2_hbm.at[pl.ds(pl.multiple_of(tok * SUBLANES, SUBLANES), SUBLANES)],
            xbuf.at[pl.ds(pl.multiple_of(g * rows + r * SUBLANES, SUBLANES), SUBLANES)],
            gsem.at[g])

    def scatter_copy(s_idx, s, r):
        dst = idx_smem[s_idx, r]
        return pltpu.make_async_copy(
            ybuf.at[pl.ds(pl.multiple_of(s * rows + r * SUBLANES, SUBLANES), SUBLANES)],
            y4_hbm.at[pl.ds(pl.multiple_of(dst * SUBLANES, SUBLANES), SUBLANES)],
            ssem.at[s])

    def wait_gather(g):
        pltpu.make_async_copy(h2_hbm.at[pl.ds(0, rows)], xbuf.at[pl.ds(g * rows, rows)],
                              gsem.at[g]).wait()

    def wait_scatter(s):
        pltpu.make_async_copy(ybuf.at[pl.ds(s * rows, rows)], y4_hbm.at[pl.ds(0, rows)],
                              ssem.at[s]).wait()

    @pl.when(i == 0)
    def _():
        ybuf[...] = jnp.zeros(ybuf.shape, F32)
        spare = pltpu.make_async_copy(
            ybuf.at[pl.ds(0, DUMP_BLOCKS * rows)],
            y4_hbm.at[pl.ds(n_asg * SUBLANES, DUMP_BLOCKS * rows)], zsem)
        spare.start()
        spare.wait()

        def fill(r, carry):
            idx_smem[3, r] = n_asg + 2 * BM + r
            return carry
        lax.fori_loop(0, BM, fill, 0)
        idx_copy(0, 0).start()
        idx_copy(0, 0).wait()

        def body(r, carry):
            gather_copy(0, 0, r).start()
            return carry
        lax.fori_loop(0, BM, body, 0)
        idx_copy(1, 1).start()

    prev_e = be_ref[jnp.maximum(i - 1, 0)]

    @pl.when((i < nb) & ((i == 0) | (be_ref[i] != prev_e)))
    def _():
        wgu_bf[...] = wgu_ref[0, 0].astype(BF16)
        wd_bf[...] = wd_ref[0, 0].astype(BF16)

    @pl.when((i < nb) & (i >= 2))
    def _():
        wait_scatter(s_cur)

    @pl.when((i < nb) & (i + 2 < nb_max))
    def _():
        idx_copy(i + 2, q_next2).start()

    @pl.when(i < nb)
    def _():
        wait_gather(g_cur)
        idx_copy(i + 1, q_next).wait()
        n_chunk = d_ff // MXU_DEPTH
        halves = MXU_DEPTH // LANES
        per_gather = -(-BM // (n_chunk - 1))
        per_scatter = BM // (2 * n_chunk)
        act_base = DUMP_BLOCKS * rows
        bgu = bgu_ref[0, 0]

        def scatter_group(g):
            for r in range(g * per_scatter, (g + 1) * per_scatter):
                scatter_copy(q_prev, s_prev, r).start(priority=r % 2)

        for c in range(n_chunk):
            xb = _load_row_tiled(xbuf, g_cur * rows, BM).astype(BF16)
            if c < n_chunk - 1:
                for r in range(c * per_gather, min((c + 1) * per_gather, BM)):
                    gather_copy(q_next, 1 - g_cur, r).start(priority=r % 2)
            gcols = slice(c * MXU_DEPTH, (c + 1) * MXU_DEPTH)
            ucols = slice(d_ff + c * MXU_DEPTH, d_ff + (c + 1) * MXU_DEPTH)
            gt = jnp.dot(xb, wgu_bf[:, gcols], preferred_element_type=F32) + bgu[:, gcols]
            up = jnp.dot(xb, wgu_bf[:, ucols], preferred_element_type=F32) + bgu[:, ucols]
            gt = jnp.minimum(gt, SWIGLU_LIMIT)
            up = jnp.clip(up, -SWIGLU_LIMIT, SWIGLU_LIMIT)
            act_c = (up + 1.0) * gt * _sigmoid(gt * SWIGLU_ALPHA)
            scatter_group(c)
            for h in range(halves):
                ybuf[pl.ds(act_base + (c * halves + h) * BM, BM), :] = act_c[:, h * LANES:(h + 1) * LANES]
        act = jnp.concatenate(
            [ybuf[pl.ds(act_base + j * BM, BM), :] for j in range(n_chunk * halves)],
            axis=-1).astype(BF16)
        for c in range(n_chunk):
            cols = slice(c * MXU_DEPTH, (c + 1) * MXU_DEPTH)
            yc = jnp.dot(act, wd_bf[:, cols], preferred_element_type=F32) + bd_ref[0, 0][:, cols]
            scatter_group(n_chunk + c)
            for h in range(halves):
                ybuf[pl.ds(s_cur * rows + c * halves + h, BM, stride=SUBLANES), :] = (
                    yc[:, h * LANES:(h + 1) * LANES])

    @pl.when(i == nb)
    def _():
        wait_gather(g_cur)

        @pl.when(i + 1 < nb_max)
        def _():
            idx_copy(i + 1, q_next).wait()

        @pl.when(i >= 2)
        def _():
            wait_scatter(s_cur)

        def body(r, carry):
            scatter_copy(q_prev, s_prev, r).start()
            return carry
        lax.fori_loop(0, BM, body, 0)
        wait_scatter(s_next)
        wait_scatter(s_prev)


def _expert_call(layer, block_e, n_blocks, inv, h2, w_gu, b_gu, w_down, b_down):
    n_tok = h2.shape[0] // SUBLANES
    depth, n_exp, d, two_ff = w_gu.shape
    d_ff = two_ff // 2
    nb_max = inv.shape[0] // BM
    rows = BM * SUBLANES
    grid_spec = pltpu.PrefetchScalarGridSpec(
        num_scalar_prefetch=2,
        grid=(nb_max,),
        in_specs=[
            pl.BlockSpec(memory_space=pl.ANY),
            pl.BlockSpec(memory_space=pl.ANY),
            pl.BlockSpec((1, 1, d, two_ff), lambda i, be, nb: (layer, be[i], 0, 0)),
            pl.BlockSpec((1, 1, 1, two_ff), lambda i, be, nb: (layer, be[i], 0, 0)),
            pl.BlockSpec((1, 1, d_ff, d), lambda i, be, nb: (layer, be[i], 0, 0)),
            pl.BlockSpec((1, 1, 1, d), lambda i, be, nb: (layer, be[i], 0, 0)),
        ],
        out_specs=pl.BlockSpec(memory_space=pl.ANY),
        scratch_shapes=[
            pltpu.SMEM((4, BM), jnp.int32),
            pltpu.VMEM((2 * rows, LANES), F32),
            pltpu.VMEM(((DUMP_BLOCKS + 1) * rows, LANES), F32),
            pltpu.VMEM((d, two_ff), BF16),
            pltpu.VMEM((d_ff, d), BF16),
            pltpu.SemaphoreType.DMA((4,)),
            pltpu.SemaphoreType.DMA((2,)),
            pltpu.SemaphoreType.DMA((3,)),
            pltpu.SemaphoreType.DMA(()),
        ],
    )
    return pl.pallas_call(
        functools.partial(_expert_kernel, nb_max=nb_max, n_tok=n_tok),
        grid_spec=grid_spec,
        out_shape=jax.ShapeDtypeStruct(((TOP_K * n_tok + DUMP_BLOCKS * BM) * SUBLANES, LANES), F32),
        compiler_params=pltpu.CompilerParams(dimension_semantics=("arbitrary",),
                                             vmem_limit_bytes=VMEM_LIMIT),
        name="experts",
    )(block_e, n_blocks, inv, h2, w_gu, b_gu.reshape(depth, n_exp, 1, two_ff), w_down,
      b_down.reshape(depth, n_exp, 1, d))


def kernel(x, c, ada_w, ada_b, mix_pre_g, mix_post_g, w_in, sc_conv, sc_out, cf_conv, cf_conv_b, cf_ln_g, cf_ln_b, cf_out, cf_out_b, pool_w, pool_scale, w_o, moe_pre_g, moe_post_g, w_router, b_router, w_gu, b_gu, w_down, b_down):
    bsz, seq_len, d = x.shape
    depth = ada_w.shape[0]
    n_tok = bsz * seq_len
    assert d == SUBLANES * LANES and seq_len % TM == 0 and d % COL_CHUNK == 0 and TM % ROW_CHUNK == 0
    assert (n_tok * TOP_K) % (ROUTE_ROWS * LANES) == 0 and (n_tok * TOP_K) % BM == 0
    assert n_tok >= 2 * BM and n_tok & (n_tok - 1) == 0 and BM & (BM - 1) == 0
    mods = _ada_call(c, ada_w, ada_b)
    xs = x.reshape(n_tok, d)
    comb = None
    for l in range(depth):
        p = dict(
            pre_g=mix_pre_g[l], post_g=mix_post_g[l], w_in=w_in[l].astype(BF16), sc_conv=sc_conv[l],
            sc_out=sc_out[l].astype(BF16), cf_conv=cf_conv[l], cf_conv_b=cf_conv_b[l], cf_ln_g=cf_ln_g[l],
            cf_ln_b=cf_ln_b[l], cf_out=cf_out[l].astype(BF16), cf_out_b=cf_out_b[l],
            pool_w=pool_w[l].astype(BF16), pool_scale=pool_scale[l], w_o=w_o[l].astype(BF16),
            moe_pre_g=moe_pre_g[l], w_router_t=_split_bf16(w_router[l].T), b_router=b_router[l],
        )
        x1, h2, eidx, gate, rank, counts = _mix_call(xs if comb is None else None, comb, mods[l], p, seq_len)
        inv, block_e, n_blocks = _routing_tables(eidx, rank, counts[:, 0])
        y4 = _expert_call(l, block_e, n_blocks, inv, h2, w_gu, b_gu, w_down, b_down)
        comb = (x1, y4, gate.T, mods[l], moe_post_g[l])
    return _final_call(comb, seq_len).reshape(bsz, seq_len, d)
```

```python
import functools

import jax
import jax.numpy as jnp
from jax import lax
from jax.experimental import pallas as pl
from jax.experimental.pallas import tpu as pltpu

F32 = jnp.float32
BF16 = jnp.bfloat16

N_EXPERTS = 32
TOP_K = 4
N_MOD = 6
SC_WIDTH = 3
CF_WIDTH = 31
POOL_WINDOWS = (2, 4, 8, 16)
SWIGLU_LIMIT = 7.0
SWIGLU_ALPHA = 1.702
NORM_EPS = 1e-6
LN_EPS = 1e-5

SUBLANES = 8
LANES = 128
MXU_DEPTH = 256
TM = 256
BM = 256
SC_HALO = 8
CF_HALO = 32
POOL_HALO = 16
ROW_CHUNK = 32
COL_CHUNK = 256
ROUTE_ROWS = 16
DUMP_BLOCKS = 3
LOOKAHEAD_BLOCKS = 2
VMEM_LIMIT = 60 * 1024 * 1024


def _rms(x, g):
    return x * lax.rsqrt(jnp.mean(x * x, axis=-1, keepdims=True) + NORM_EPS) * g


def _sigmoid(x):
    return 0.5 * jnp.tanh(0.5 * x) + 0.5


def _const_spec(shape):
    nd = len(shape)
    return pl.BlockSpec(shape, lambda *_: (0,) * nd, pipeline_mode=pl.Buffered(1))


def _load_row_tiled(ref, base, rows):
    return jnp.concatenate(
        [ref[pl.ds(base + s, rows, stride=SUBLANES), :] for s in range(SUBLANES)], axis=-1)


def _store_row_tiled(ref, base, value):
    rows = value.shape[0]
    for s in range(SUBLANES):
        ref[pl.ds(base + s, rows, stride=SUBLANES), :] = value[:, s * LANES:(s + 1) * LANES]


def _ada_kernel(c_ref, w_ref, b_ref, o_ref):
    c = c_ref[...]
    c_act = c * _sigmoid(c)
    o_ref[0, 0] = jnp.dot(c_act, w_ref[0], preferred_element_type=F32,
                          precision=lax.Precision.HIGHEST) + b_ref[0, 0]


def _ada_call(c, ada_w, ada_b):
    depth, d, _ = ada_w.shape
    bsz = c.shape[0]
    ada_b4 = ada_b.reshape(depth, N_MOD, 1, d)
    out = pl.pallas_call(
        _ada_kernel,
        grid=(depth, N_MOD),
        in_specs=[
            pl.BlockSpec((bsz, d), lambda l, j: (0, 0)),
            pl.BlockSpec((1, d, d), lambda l, j: (l, 0, j)),
            pl.BlockSpec((1, 1, 1, d), lambda l, j: (l, j, 0, 0)),
        ],
        out_specs=pl.BlockSpec((1, 1, bsz, d), lambda l, j: (l, j, 0, 0)),
        out_shape=jax.ShapeDtypeStruct((depth, N_MOD, bsz, d), F32),
        name="ada_mod",
    )(c, ada_w, ada_b4)
    return jnp.transpose(out, (0, 2, 1, 3))


def _combine(x1_ref, y4_refs, gate_ref, mod_ref, post_g):
    tm = x1_ref.shape[0]
    gate = gate_ref[...]
    y = None
    for k, y_ref in enumerate(y4_refs):
        term = gate[:, k:k + 1] * _load_row_tiled(y_ref, 0, tm)
        y = term if y is None else y + term
    return x1_ref[...] + mod_ref[0, 5:6, :] * _rms(y, post_g[...])


def _combine_specs(n, d, steps_per_seq):
    blocks_per_plane = n // TM
    specs = [pl.BlockSpec((TM, d), lambda i: (i, 0))]
    for k in range(TOP_K):
        specs.append(pl.BlockSpec((TM * SUBLANES, LANES),
                                  lambda i, k=k: (k * blocks_per_plane + i, 0)))
    specs += [
        pl.BlockSpec((TM, TOP_K), lambda i: (i, 0)),
        pl.BlockSpec((1, N_MOD, d), lambda i: (i // steps_per_seq, 0, 0)),
        pl.BlockSpec((1, d), lambda i: (0, 0)),
    ]
    return specs


def _final_kernel(x1_ref, y0, y1, y2, y3, gate_ref, mod_ref, post_g, out_ref):
    out_ref[...] = _combine(x1_ref, (y0, y1, y2, y3), gate_ref, mod_ref, post_g)


def _final_call(comb, seq_len):
    x1, y4, gate_nk, mod, post_g = comb
    n, d = x1.shape
    return pl.pallas_call(
        _final_kernel,
        grid=(n // TM,),
        in_specs=_combine_specs(n, d, seq_len // TM),
        out_specs=pl.BlockSpec((TM, d), lambda i: (i, 0)),
        out_shape=jax.ShapeDtypeStruct((n, d), F32),
        compiler_params=pltpu.CompilerParams(dimension_semantics=("arbitrary",),
                                             vmem_limit_bytes=VMEM_LIMIT),
        name="final_combine",
    )(x1, y4, y4, y4, y4, gate_nk, mod, post_g.reshape(1, d))


def _causal_conv(buf_ref, w_ref, width, halo, out_ref, shift_ref, bias_ref=None, before_chunk=()):
    tm, d = out_ref.shape
    n_rows = buf_ref.shape[0] - SUBLANES
    offsets = [halo - (width - 1) + k for k in range(width)]
    for ci, c0 in enumerate(range(0, d, COL_CHUNK)):
        if ci < len(before_chunk):
            before_chunk[ci]()
        cols = slice(c0, c0 + COL_CHUNK)
        for q in sorted({o % SUBLANES for o in offsets} - {0}):
            shift_ref[q - 1, 0:n_rows, :] = buf_ref[q:q + n_rows, cols]
        for r0 in range(0, tm, ROW_CHUNK):
            accs = [None] * (ROW_CHUNK // SUBLANES)
            for k, o in enumerate(offsets):
                q = o % SUBLANES
                w8 = w_ref[k, :, cols]
                for j in range(len(accs)):
                    start = o - q + r0 + j * SUBLANES
                    if q == 0:
                        rows = buf_ref[start:start + SUBLANES, cols]
                    else:
                        rows = shift_ref[q - 1, start:start + SUBLANES, :]
                    accs[j] = w8 * rows if accs[j] is None else accs[j] + w8 * rows
            for j, acc in enumerate(accs):
                if bias_ref is not None:
                    acc = acc + bias_ref[:, cols]
                out_ref[r0 + j * SUBLANES:r0 + (j + 1) * SUBLANES, cols] = acc


def _mix_kernel(*refs, steps_per_seq, has_comb):
    n_in = 8 if has_comb else 1
    x_in = refs[:n_in]
    (mod_ref, pre_g, post_g, w_in, sc_conv, sc_out, cf_conv, cf_conv_b, cf_ln_g, cf_ln_b, cf_out,
     cf_out_b, pool_w, pool_scale, w_o, moe_pre_g, w_router_t, b_router, band_cur, band_prev,
     x1_ref, h2_ref, eidx_ref, gate_ref, rank_ref, counts_ref,
     cx_buf, v_buf, u_buf, tmp_buf, tmp2_buf, shift_buf, carry_ref) = refs[n_in:]
    i = pl.program_id(0)
    t = i % steps_per_seq
    tm, d = x1_ref.shape

    @pl.when(t == 0)
    def _():
        cx_buf[0:SC_HALO, :] = jnp.zeros((SC_HALO, d), F32)
        v_buf[0:CF_HALO, :] = jnp.zeros((CF_HALO, d), F32)
        u_buf[...] = jnp.zeros((POOL_HALO, d), F32)

    @pl.when(i == 0)
    def _():
        carry_ref[...] = jnp.zeros(carry_ref.shape, F32)

    if has_comb:
        xp_ref, y0, y1, y2, y3, gate_prev, mod_prev, post_g_prev = x_in
        x = _combine(xp_ref, (y0, y1, y2, y3), gate_prev, mod_prev, post_g_prev)
    else:
        x = x_in[0][...]
    shift1, scale1, gate1 = mod_ref[0, 0:1, :], mod_ref[0, 1:2, :], mod_ref[0, 2:3, :]
    shift2, scale2 = mod_ref[0, 3:4, :], mod_ref[0, 4:5, :]

    hb = (_rms(x, pre_g[...]) * (1.0 + scale1) + shift1).astype(BF16)

    def proj(j):
        return jnp.dot(hb, w_in[:, j * d:(j + 1) * d], preferred_element_type=F32)

    v_buf[CF_HALO:CF_HALO + tm, :] = proj(3) * _sigmoid(proj(4))
    side = {}

    def pool_in():
        side["u"] = proj(5)

    def branch_a():
        cx_buf[SC_HALO:SC_HALO + tm, :] = proj(1) * proj(2)
        _causal_conv(cx_buf, sc_conv, SC_WIDTH, SC_HALO, tmp2_buf, shift_buf)
        cx_buf[0:SC_HALO, :] = cx_buf[tm:tm + SC_HALO, :]
        conv_a = (proj(0) * tmp2_buf[...]).astype(BF16)
        side["merged"] = _sigmoid(proj(6)) * jnp.dot(conv_a, sc_out[...], preferred_element_type=F32)

    def gate_b():
        side["gate_b"] = _sigmoid(proj(7))

    def gate_c():
        side["gate_c"] = _sigmoid(proj(8))

    _causal_conv(v_buf, cf_conv, CF_WIDTH, CF_HALO, tmp_buf, shift_buf, cf_conv_b,
                 before_chunk=(pool_in, branch_a, gate_b, gate_c))
    v_buf[0:CF_HALO, :] = v_buf[tm:tm + CF_HALO, :]
    v = tmp_buf[...]
    mu = jnp.mean(v, axis=-1, keepdims=True)
    vc = v - mu
    var = jnp.mean(vc * vc, axis=-1, keepdims=True)
    vn = vc * lax.rsqrt(var + LN_EPS) * cf_ln_g[...] + cf_ln_b[...]
    vn = vn * _sigmoid(vn)
    y_b = jnp.dot(vn.astype(BF16), cf_out[...], preferred_element_type=F32) + cf_out_b[...]
    merged = side["merged"] + side["gate_b"] * y_b

    u = side["u"]
    pos = (t * tm + 1 + lax.broadcasted_iota(jnp.int32, (tm, 1), 0)).astype(F32)
    group = d // len(POOL_WINDOWS)

    def split(a):
        hi = a.astype(BF16)
        return jnp.concatenate([hi, (a - hi.astype(F32)).astype(BF16)], axis=-1)

    y_c_parts = []
    for g, w in enumerate(POOL_WINDOWS):
        cols = slice(g * group, (g + 1) * group)
        s2 = jnp.dot(band_cur[g], split(u[:, cols]), preferred_element_type=F32)
        h2s = jnp.dot(band_prev[g], split(u_buf[:, cols]), preferred_element_type=F32)
        s = s2[:, :group] + s2[:, group:]
        s = jnp.concatenate([s[:POOL_HALO] + h2s[:, :group] + h2s[:, group:], s[POOL_HALO:]], axis=0)
        p = s * (1.0 / jnp.minimum(pos, float(w))) - u[:, cols]
        y_c_parts.append(jnp.dot(p.astype(BF16), pool_w[g], preferred_element_type=F32))
    u_buf[...] = u[tm - POOL_HALO:, :]
    y_c = jnp.concatenate(y_c_parts, axis=-1) * pool_scale[...]
    merged = merged + side["gate_c"] * y_c

    y = jnp.dot(merged.astype(BF16), w_o[...], preferred_element_type=F32)
    x1 = x + gate1 * _rms(y, post_g[...])
    x1_ref[...] = x1

    h2 = _rms(x1, moe_pre_g[...]) * (1.0 + scale2) + shift2
    _store_row_tiled(h2_ref, 0, h2)
    h2_hi = h2.astype(BF16)
    h2_split = jnp.concatenate([h2_hi, (h2 - h2_hi.astype(F32)).astype(BF16)], axis=0)
    cross = lax.dot_general(w_router_t[...], h2_split, (((1,), (1,)), ((), ())),
                            preferred_element_type=F32)
    logits = (cross[:N_EXPERTS, :tm] + cross[N_EXPERTS:, :tm] + cross[:N_EXPERTS, tm:]
              + b_router[...])
    eio = lax.broadcasted_iota(jnp.int32, (N_EXPERTS, tm), 0)
    neg_inf = jnp.float32(-jnp.inf)
    vals, idxs = [], []
    for _ in range(TOP_K):
        m = jnp.max(logits, axis=0, keepdims=True)
        idx = jnp.min(jnp.where(logits == m, eio, N_EXPERTS), axis=0, keepdims=True)
        vals.append(m)
        idxs.append(idx)
        logits = jnp.where(eio == idx, neg_inf, logits)
    exps = [jnp.exp(v - vals[0]) for v in vals]
    denom = exps[0] + exps[1] + exps[2] + exps[3]
    onehot = jnp.zeros((N_EXPERTS, tm), F32)
    for idx in idxs:
        onehot = onehot + (eio == idx).astype(F32)
    tri = (lax.broadcasted_iota(jnp.int32, (tm, tm), 0)
           < lax.broadcasted_iota(jnp.int32, (tm, tm), 1)).astype(BF16)
    carry = carry_ref[:, 0:1]
    before = jnp.dot(onehot.astype(BF16), tri, preferred_element_type=F32) + carry
    for k in range(TOP_K):
        eidx_ref[k:k + 1, :] = idxs[k]
        gate_ref[k:k + 1, :] = exps[k] / denom
        rank_ref[k:k + 1, :] = jnp.sum(jnp.where(eio == idxs[k], before, 0.0), axis=0,
                                       keepdims=True).astype(jnp.int32)
    carry = carry + jnp.sum(onehot, axis=1, keepdims=True)
    carry_ref[...] = jnp.broadcast_to(carry, carry_ref.shape)
    counts_ref[...] = jnp.broadcast_to(carry, counts_ref.shape).astype(jnp.int32)


def _split_bf16(a):
    hi = a.astype(BF16)
    return jnp.concatenate([hi, (a - hi.astype(F32)).astype(BF16)], axis=0)


def _pool_bands():
    t = jnp.arange(TM)[:, None]
    j = jnp.arange(TM)[None, :]
    th = jnp.arange(POOL_HALO)[:, None]
    jh = jnp.arange(POOL_HALO)[None, :] - POOL_HALO
    cur = jnp.stack([(j <= t) & (j > t - w) for w in POOL_WINDOWS]).astype(BF16)
    prev = jnp.stack([jh > th - w for w in POOL_WINDOWS]).astype(BF16)
    return cur, prev


def _mix_call(x2d, comb, mod, p, seq_len):
    has_comb = comb is not None
    n, d = comb[0].shape if has_comb else x2d.shape
    steps_per_seq = seq_len // TM
    n_steps = n // TM
    row = lambda a: a.reshape(1, d)
    taps = lambda a: jnp.broadcast_to(a[:, None, :], (a.shape[0], SUBLANES, d))
    if has_comb:
        x1p, y4p, gate_p, mod_p, post_g_p = comb
        lead_args = (x1p, y4p, y4p, y4p, y4p, gate_p, mod_p, row(post_g_p))
        lead_specs = _combine_specs(n, d, steps_per_seq)
    else:
        lead_args = (x2d,)
        lead_specs = [pl.BlockSpec((TM, d), lambda i: (i, 0))]
    const_args = (
        row(p["pre_g"]), row(p["post_g"]), p["w_in"], taps(p["sc_conv"]), p["sc_out"], taps(p["cf_conv"]),
        row(p["cf_conv_b"]), row(p["cf_ln_g"]), row(p["cf_ln_b"]), p["cf_out"], row(p["cf_out_b"]),
        p["pool_w"], row(p["pool_scale"]), p["w_o"], row(p["moe_pre_g"]), p["w_router_t"],
        p["b_router"].reshape(N_EXPERTS, 1), *_pool_bands(),
    )
    in_specs = (lead_specs + [pl.BlockSpec((1, N_MOD, d), lambda i: (i // steps_per_seq, 0, 0))]
                + [_const_spec(a.shape) for a in const_args])
    tok_spec = pl.BlockSpec((TOP_K, TM), lambda i: (0, i))
    out_shape = (
        jax.ShapeDtypeStruct((n, d), F32),
        jax.ShapeDtypeStruct((n * SUBLANES, LANES), F32),
        jax.ShapeDtypeStruct((TOP_K, n), jnp.int32),
        jax.ShapeDtypeStruct((TOP_K, n), F32),
        jax.ShapeDtypeStruct((TOP_K, n), jnp.int32),
        jax.ShapeDtypeStruct((N_EXPERTS, LANES), jnp.int32),
    )
    out_specs = (
        pl.BlockSpec((TM, d), lambda i: (i, 0)),
        pl.BlockSpec((TM * SUBLANES, LANES), lambda i: (i, 0)),
        tok_spec, tok_spec, tok_spec,
        pl.BlockSpec((N_EXPERTS, LANES), lambda i: (0, 0)),
    )
    return pl.pallas_call(
        functools.partial(_mix_kernel, steps_per_seq=steps_per_seq, has_comb=has_comb),
        grid=(n_steps,),
        in_specs=in_specs,
        out_specs=out_specs,
        out_shape=out_shape,
        scratch_shapes=[
            pltpu.VMEM((SC_HALO + TM, d), F32),
            pltpu.VMEM((CF_HALO + TM, d), F32),
            pltpu.VMEM((POOL_HALO, d), F32),
            pltpu.VMEM((TM, d), F32),
            pltpu.VMEM((TM, d), F32),
            pltpu.VMEM((SUBLANES - 1, CF_HALO + TM - SUBLANES, COL_CHUNK), F32),
            pltpu.VMEM((N_EXPERTS, LANES), F32),
        ],
        compiler_params=pltpu.CompilerParams(dimension_semantics=("arbitrary",),
                                             vmem_limit_bytes=VMEM_LIMIT),
        name="mixer_router",
    )(*lead_args, mod, *const_args)


def _route_kernel(pstart_ref, pvalid_ref, pend_ref, eidx_ref, rank_ref, inv_ref,
                  dest_vmem, dest_smem, sem, *, n_chunks, n_asg):
    c = pl.program_id(0)
    n_pad = inv_ref.shape[0]
    chunk = ROUTE_ROWS * LANES

    def to_smem(s):
        return pltpu.make_async_copy(dest_vmem.at[s], dest_smem.at[s], sem.at[s])

    @pl.when(c == 0)
    def _():
        def pad_slot(p, carry):
            inv_ref[p] = n_asg + (p & (2 * BM - 1))
            return carry

        def pad_expert(e, carry):
            lax.fori_loop(pvalid_ref[e], pend_ref[e], pad_slot, 0)
            return carry

        lax.fori_loop(0, N_EXPERTS, pad_expert, 0)
        lax.fori_loop(pend_ref[N_EXPERTS - 1], n_pad, pad_slot, 0)

    @pl.when(c < n_chunks)
    def _():
        eidx = eidx_ref[...]
        dest = rank_ref[...]
        for e in range(N_EXPERTS):
            dest = dest + jnp.where(eidx == e, pstart_ref[e], 0)
        dest_vmem[c % 2] = dest
        to_smem(c % 2).start()

    @pl.when(c >= 1)
    def _():
        s = (c - 1) % 2
        to_smem(s).wait()
        base = (c - 1) * chunk

        def row(r, carry):
            for col in range(LANES):
                inv_ref[dest_smem[s, r, col]] = base + r * LANES + col
            return carry

        lax.fori_loop(0, ROUTE_ROWS, row, 0)


def _routing_tables(eidx, rank, counts):
    n = eidx.shape[1]
    n_asg = n * TOP_K
    n_pad = n_asg + N_EXPERTS * BM
    nb_max = n_pad // BM
    chunk = ROUTE_ROWS * LANES
    n_chunks = n_asg // chunk
    padded = (counts + BM - 1) // BM * BM
    pend = jnp.cumsum(padded).astype(jnp.int32)
    pstart = pend - padded
    n_blocks = pend[-1:] // BM
    blk = jnp.arange(nb_max, dtype=jnp.int32)
    block_e = jnp.minimum(jnp.sum((pend[None, :] <= blk[:, None] * BM).astype(jnp.int32), axis=1),
                          N_EXPERTS - 1)
    last_e = jnp.sum(jnp.where(blk == n_blocks[0] - 1, block_e, 0))
    block_e = jnp.where(blk < n_blocks[0], block_e, last_e).astype(jnp.int32)
    smem = pl.BlockSpec(memory_space=pltpu.SMEM)
    tile = pl.BlockSpec((ROUTE_ROWS, LANES), lambda c: (jnp.minimum(c, n_chunks - 1), 0))
    inv = pl.pallas_call(
        functools.partial(_route_kernel, n_chunks=n_chunks, n_asg=n_asg),
        grid=(n_chunks + 1,),
        in_specs=[smem, smem, smem, tile, tile],
        out_specs=smem,
        out_shape=jax.ShapeDtypeStruct((n_pad + LOOKAHEAD_BLOCKS * BM,), jnp.int32),
        scratch_shapes=[
            pltpu.VMEM((2, ROUTE_ROWS, LANES), jnp.int32),
            pltpu.SMEM((2, ROUTE_ROWS, LANES), jnp.int32),
            pltpu.SemaphoreType.DMA((2,)),
        ],
        compiler_params=pltpu.CompilerParams(dimension_semantics=("arbitrary",)),
        name="route_tables",
    )(pstart, pstart + counts, pend, eidx.reshape(n_asg // LANES, LANES),
      rank.reshape(n_asg // LANES, LANES))
    return inv, block_e, n_blocks


def _expert_kernel(be_ref, nb_ref, inv_hbm, h2_hbm, wgu_ref, bgu_ref, wd_ref, bd_ref, y4_hbm,
                   idx_smem, xbuf, ybuf, wgu_bf, wd_bf, isem, gsem, ssem, zsem,
                   *, nb_max, n_tok):
    i = pl.program_id(0)
    nb = nb_ref[0]
    s_cur = i % 3
    s_next = (i + 1) % 3
    s_prev = (i + 2) % 3
    q_ahead2 = (i + 2) % 5
    q_ahead3 = (i + 3) % 5
    q_prev = (i + 4) % 5
    rows = BM * SUBLANES
    d_ff = wd_bf.shape[0]
    n_asg = TOP_K * n_tok

    def idx_copy(blk, s):
        return pltpu.make_async_copy(inv_hbm.at[pl.ds(blk * BM, BM)], idx_smem.at[s], isem.at[s])

    def gather_copy(s_idx, g, r):
        tok = idx_smem[s_idx, r] & (n_tok - 1)
        return pltpu.make_async_copy(
            h2_hbm.at[pl.ds(pl.multiple_of(tok * SUBLANES, SUBLANES), SUBLANES)],
            xbuf.at[pl.ds(pl.multiple_of(g * rows + r * SUBLANES, SUBLANES), SUBLANES)],
            gsem.at[g])

    def scatter_copy(s_idx, s, r):
        dst = idx_smem[s_idx, r]
        return pltpu.make_async_copy(
            ybuf.at[pl.ds(pl.multiple_of(s * rows + r * SUBLANES, SUBLANES), SUBLANES)],
            y4_hbm.at[pl.ds(pl.multiple_of(dst * SUBLANES, SUBLANES), SUBLANES)],
            ssem.at[s])

    def wait_gather(g):
        pltpu.make_async_copy(h2_hbm.at[pl.ds(0, rows)], xbuf.at[pl.ds(g * rows, rows)],
                              gsem.at[g]).wait()

    def wait_scatter(s):
        pltpu.make_async_copy(ybuf.at[pl.ds(s * rows, rows)], y4_hbm.at[pl.ds(0, rows)],
                              ssem.at[s]).wait()

    @pl.when(i == 0)
    def _():
        ybuf[...] = jnp.zeros(ybuf.shape, F32)
        spare = pltpu.make_async_copy(
            ybuf.at[pl.ds(0, DUMP_BLOCKS * rows)],
            y4_hbm.at[pl.ds(n_asg * SUBLANES, DUMP_BLOCKS * rows)], zsem)
        spare.start()
        spare.wait()

        def fill(r, carry):
            idx_smem[4, r] = n_asg + 2 * BM + r
            return carry
        lax.fori_loop(0, BM, fill, 0)
        for b in range(2):
            idx_copy(b, b).start()
            idx_copy(b, b).wait()

            def body(r, carry, b=b):
                gather_copy(b, b, r).start()
                return carry
            lax.fori_loop(0, BM, body, 0)
        idx_copy(2, 2).start()

    prev_e = be_ref[jnp.maximum(i - 1, 0)]

    @pl.when((i < nb) & ((i == 0) | (be_ref[i] != prev_e)))
    def _():
        wgu_bf[...] = wgu_ref[0, 0].astype(BF16)
        wd_bf[...] = wd_ref[0, 0].astype(BF16)

    @pl.when((i < nb) & (i >= 2))
    def _():
        wait_scatter(s_cur)

    @pl.when(i < nb)
    def _():
        idx_copy(i + 3, q_ahead3).start()
        wait_gather(s_cur)
        idx_copy(i + 2, q_ahead2).wait()
        n_chunk = d_ff // MXU_DEPTH
        halves = MXU_DEPTH // LANES
        per_gather = -(-BM // (n_chunk - 1))
        per_scatter = BM // (2 * n_chunk)
        act_base = DUMP_BLOCKS * rows
        bgu = bgu_ref[0, 0]

        def scatter_group(g):
            for r in range(g * per_scatter, (g + 1) * per_scatter):
                scatter_copy(q_prev, s_prev, r).start(priority=r % 2)

        for c in range(n_chunk):
            xb = _load_row_tiled(xbuf, s_cur * rows, BM).astype(BF16)
            gcols = slice(c * MXU_DEPTH, (c + 1) * MXU_DEPTH)
            ucols = slice(d_ff + c * MXU_DEPTH, d_ff + (c + 1) * MXU_DEPTH)
            gt = jnp.dot(xb, wgu_bf[:, gcols], preferred_element_type=F32) + bgu[:, gcols]
            up = jnp.dot(xb, wgu_bf[:, ucols], preferred_element_type=F32) + bgu[:, ucols]
            gt = jnp.minimum(gt, SWIGLU_LIMIT)
            up = jnp.clip(up, -SWIGLU_LIMIT, SWIGLU_LIMIT)
            act_c = (up + 1.0) * gt * _sigmoid(gt * SWIGLU_ALPHA)
            if c < n_chunk - 1:
                for r in range(c * per_gather, min((c + 1) * per_gather, BM)):
                    gather_copy(q_ahead2, s_prev, r).start(priority=r % 2)
            scatter_group(c)
            for h in range(halves):
                ybuf[pl.ds(act_base + (c * halves + h) * BM, BM), :] = act_c[:, h * LANES:(h + 1) * LANES]
        act = jnp.concatenate(
            [ybuf[pl.ds(act_base + j * BM, BM), :] for j in range(n_chunk * halves)],
            axis=-1).astype(BF16)
        for c in range(n_chunk):
            cols = slice(c * MXU_DEPTH, (c + 1) * MXU_DEPTH)
            yc = jnp.dot(act, wd_bf[:, cols], preferred_element_type=F32) + bd_ref[0, 0][:, cols]
            scatter_group(n_chunk + c)
            for h in range(halves):
                ybuf[pl.ds(s_cur * rows + c * halves + h, BM, stride=SUBLANES), :] = (
                    yc[:, h * LANES:(h + 1) * LANES])

    @pl.when(i == nb)
    def _():
        wait_gather(s_cur)
        wait_gather(s_next)
        idx_copy(i + 2, q_ahead2).wait()

        @pl.when(i >= 2)
        def _():
            wait_scatter(s_cur)

        def body(r, carry):
            scatter_copy(q_prev, s_prev, r).start()
            return carry
        lax.fori_loop(0, BM, body, 0)
        wait_scatter(s_next)
        wait_scatter(s_prev)


def _expert_call(layer, block_e, n_blocks, inv, h2, w_gu, b_gu, w_down, b_down):
    n_tok = h2.shape[0] // SUBLANES
    depth, n_exp, d, two_ff = w_gu.shape
    d_ff = two_ff // 2
    nb_max = inv.shape[0] // BM - LOOKAHEAD_BLOCKS
    rows = BM * SUBLANES
    grid_spec = pltpu.PrefetchScalarGridSpec(
        num_scalar_prefetch=2,
        grid=(nb_max,),
        in_specs=[
            pl.BlockSpec(memory_space=pl.ANY),
            pl.BlockSpec(memory_space=pl.ANY),
            pl.BlockSpec((1, 1, d, two_ff), lambda i, be, nb: (layer, be[i], 0, 0)),
            pl.BlockSpec((1, 1, 1, two_ff), lambda i, be, nb: (layer, be[i], 0, 0)),
            pl.BlockSpec((1, 1, d_ff, d), lambda i, be, nb: (layer, be[i], 0, 0)),
            pl.BlockSpec((1, 1, 1, d), lambda i, be, nb: (layer, be[i], 0, 0)),
        ],
        out_specs=pl.BlockSpec(memory_space=pl.ANY),
        scratch_shapes=[
            pltpu.SMEM((5, BM), jnp.int32),
            pltpu.VMEM((3 * rows, LANES), F32),
            pltpu.VMEM(((DUMP_BLOCKS + 1) * rows, LANES), F32),
            pltpu.VMEM((d, two_ff), BF16),
            pltpu.VMEM((d_ff, d), BF16),
            pltpu.SemaphoreType.DMA((5,)),
            pltpu.SemaphoreType.DMA((3,)),
            pltpu.SemaphoreType.DMA((3,)),
            pltpu.SemaphoreType.DMA(()),
        ],
    )
    return pl.pallas_call(
        functools.partial(_expert_kernel, nb_max=nb_max, n_tok=n_tok),
        grid_spec=grid_spec,
        out_shape=jax.ShapeDtypeStruct(((TOP_K * n_tok + DUMP_BLOCKS * BM) * SUBLANES, LANES), F32),
        compiler_params=pltpu.CompilerParams(dimension_semantics=("arbitrary",),
                                             vmem_limit_bytes=VMEM_LIMIT),
        name="experts",
    )(block_e, n_blocks, inv, h2, w_gu, b_gu.reshape(depth, n_exp, 1, two_ff), w_down,
      b_down.reshape(depth, n_exp, 1, d))


def kernel(x, c, ada_w, ada_b, mix_pre_g, mix_post_g, w_in, sc_conv, sc_out, cf_conv, cf_conv_b, cf_ln_g, cf_ln_b, cf_out, cf_out_b, pool_w, pool_scale, w_o, moe_pre_g, moe_post_g, w_router, b_router, w_gu, b_gu, w_down, b_down):
    bsz, seq_len, d = x.shape
    depth = ada_w.shape[0]
    n_tok = bsz * seq_len
    assert d == SUBLANES * LANES and seq_len % TM == 0 and d % COL_CHUNK == 0 and TM % ROW_CHUNK == 0
    assert (n_tok * TOP_K) % (ROUTE_ROWS * LANES) == 0 and (n_tok * TOP_K) % BM == 0
    assert n_tok >= 2 * BM and n_tok & (n_tok - 1) == 0 and BM & (BM - 1) == 0
    mods = _ada_call(c, ada_w, ada_b)
    xs = x.reshape(n_tok, d)
    comb = None
    for l in range(depth):
        p = dict(
            pre_g=mix_pre_g[l], post_g=mix_post_g[l], w_in=w_in[l].astype(BF16), sc_conv=sc_conv[l],
            sc_out=sc_out[l].astype(BF16), cf_conv=cf_conv[l], cf_conv_b=cf_conv_b[l], cf_ln_g=cf_ln_g[l],
            cf_ln_b=cf_ln_b[l], cf_out=cf_out[l].astype(BF16), cf_out_b=cf_out_b[l],
            pool_w=pool_w[l].astype(BF16), pool_scale=pool_scale[l], w_o=w_o[l].astype(BF16),
            moe_pre_g=moe_pre_g[l], w_router_t=_split_bf16(w_router[l].T), b_router=b_router[l],
        )
        x1, h2, eidx, gate, rank, counts = _mix_call(xs if comb is None else None, comb, mods[l], p, seq_len)
        inv, block_e, n_blocks = _routing_tables(eidx, rank, counts[:, 0])
        y4 = _expert_call(l, block_e, n_blocks, inv, h2, w_gu, b_gu, w_down, b_down)
        comb = (x1, y4, gate.T, mods[l], moe_post_g[l])
    return _final_call(comb, seq_len).reshape(bsz, seq_len, d)
```

```python
import functools

import jax
import jax.numpy as jnp
from jax import lax
from jax.experimental import pallas as pl
from jax.experimental.pallas import tpu as pltpu

F32 = jnp.float32
BF16 = jnp.bfloat16

N_EXPERTS = 32
TOP_K = 4
N_MOD = 6
SC_WIDTH = 3
CF_WIDTH = 31
POOL_WINDOWS = (2, 4, 8, 16)
SWIGLU_LIMIT = 7.0
SWIGLU_ALPHA = 1.702
NORM_EPS = 1e-6
LN_EPS = 1e-5

SUBLANES = 8
LANES = 128
MXU_DEPTH = 256
TM = 256
BM = 512
SC_HALO = 8
CF_HALO = 32
POOL_HALO = 16
ROW_CHUNK = 32
COL_CHUNK = 256
ROUTE_ROWS = 16
DUMP_BLOCKS = 3
LOOKAHEAD_BLOCKS = 2
VMEM_LIMIT = 60 * 1024 * 1024


def _rms(x, g):
    return x * lax.rsqrt(jnp.mean(x * x, axis=-1, keepdims=True) + NORM_EPS) * g


def _sigmoid(x):
    return 0.5 * jnp.tanh(0.5 * x) + 0.5


def _const_spec(shape):
    nd = len(shape)
    return pl.BlockSpec(shape, lambda *_: (0,) * nd, pipeline_mode=pl.Buffered(1))


def _load_row_tiled(ref, base, rows):
    return jnp.concatenate(
        [ref[pl.ds(base + s, rows, stride=SUBLANES), :] for s in range(SUBLANES)], axis=-1)


def _store_row_tiled(ref, base, value):
    rows = value.shape[0]
    for s in range(SUBLANES):
        ref[pl.ds(base + s, rows, stride=SUBLANES), :] = value[:, s * LANES:(s + 1) * LANES]


def _ada_kernel(c_ref, w_ref, b_ref, o_ref):
    c = c_ref[...]
    c_act = c * _sigmoid(c)
    o_ref[0, 0] = jnp.dot(c_act, w_ref[0], preferred_element_type=F32,
                          precision=lax.Precision.HIGHEST) + b_ref[0, 0]


def _ada_call(c, ada_w, ada_b):
    depth, d, _ = ada_w.shape
    bsz = c.shape[0]
    ada_b4 = ada_b.reshape(depth, N_MOD, 1, d)
    out = pl.pallas_call(
        _ada_kernel,
        grid=(depth, N_MOD),
        in_specs=[
            pl.BlockSpec((bsz, d), lambda l, j: (0, 0)),
            pl.BlockSpec((1, d, d), lambda l, j: (l, 0, j)),
            pl.BlockSpec((1, 1, 1, d), lambda l, j: (l, j, 0, 0)),
        ],
        out_specs=pl.BlockSpec((1, 1, bsz, d), lambda l, j: (l, j, 0, 0)),
        out_shape=jax.ShapeDtypeStruct((depth, N_MOD, bsz, d), F32),
        name="ada_mod",
    )(c, ada_w, ada_b4)
    return jnp.transpose(out, (0, 2, 1, 3))


def _combine(x1_ref, y4_refs, gate_ref, mod_ref, post_g):
    tm = x1_ref.shape[0]
    gate = gate_ref[...]
    y = None
    for k, y_ref in enumerate(y4_refs):
        term = gate[:, k:k + 1] * _load_row_tiled(y_ref, 0, tm)
        y = term if y is None else y + term
    return x1_ref[...] + mod_ref[0, 5:6, :] * _rms(y, post_g[...])


def _combine_specs(n, d, steps_per_seq):
    blocks_per_plane = n // TM
    specs = [pl.BlockSpec((TM, d), lambda i: (i, 0))]
    for k in range(TOP_K):
        specs.append(pl.BlockSpec((TM * SUBLANES, LANES),
                                  lambda i, k=k: (k * blocks_per_plane + i, 0)))
    specs += [
        pl.BlockSpec((TM, TOP_K), lambda i: (i, 0)),
        pl.BlockSpec((1, N_MOD, d), lambda i: (i // steps_per_seq, 0, 0)),
        pl.BlockSpec((1, d), lambda i: (0, 0)),
    ]
    return specs


def _final_kernel(x1_ref, y0, y1, y2, y3, gate_ref, mod_ref, post_g, out_ref):
    out_ref[...] = _combine(x1_ref, (y0, y1, y2, y3), gate_ref, mod_ref, post_g)


def _final_call(comb, seq_len):
    x1, y4, gate_nk, mod, post_g = comb
    n, d = x1.shape
    return pl.pallas_call(
        _final_kernel,
        grid=(n // TM,),
        in_specs=_combine_specs(n, d, seq_len // TM),
        out_specs=pl.BlockSpec((TM, d), lambda i: (i, 0)),
        out_shape=jax.ShapeDtypeStruct((n, d), F32),
        compiler_params=pltpu.CompilerParams(dimension_semantics=("arbitrary",),
                                             vmem_limit_bytes=VMEM_LIMIT),
        name="final_combine",
    )(x1, y4, y4, y4, y4, gate_nk, mod, post_g.reshape(1, d))


def _causal_conv(buf_ref, w_ref, width, halo, out_ref, shift_ref, bias_ref=None, before_chunk=()):
    tm, d = out_ref.shape
    n_rows = buf_ref.shape[0] - SUBLANES
    offsets = [halo - (width - 1) + k for k in range(width)]
    for ci, c0 in enumerate(range(0, d, COL_CHUNK)):
        if ci < len(before_chunk):
            before_chunk[ci]()
        cols = slice(c0, c0 + COL_CHUNK)
        for q in sorted({o % SUBLANES for o in offsets} - {0}):
            shift_ref[q - 1, 0:n_rows, :] = buf_ref[q:q + n_rows, cols]
        for r0 in range(0, tm, ROW_CHUNK):
            accs = [None] * (ROW_CHUNK // SUBLANES)
            for k, o in enumerate(offsets):
                q = o % SUBLANES
                w8 = w_ref[k, :, cols]
                for j in range(len(accs)):
                    start = o - q + r0 + j * SUBLANES
                    if q == 0:
                        rows = buf_ref[start:start + SUBLANES, cols]
                    else:
                        rows = shift_ref[q - 1, start:start + SUBLANES, :]
                    accs[j] = w8 * rows if accs[j] is None else accs[j] + w8 * rows
            for j, acc in enumerate(accs):
                if bias_ref is not None:
                    acc = acc + bias_ref[:, cols]
                out_ref[r0 + j * SUBLANES:r0 + (j + 1) * SUBLANES, cols] = acc


def _mix_kernel(*refs, steps_per_seq, has_comb):
    n_in = 8 if has_comb else 1
    x_in = refs[:n_in]
    (mod_ref, pre_g, post_g, w_in, sc_conv, sc_out, cf_conv, cf_conv_b, cf_ln_g, cf_ln_b, cf_out,
     cf_out_b, pool_w, pool_scale, w_o, moe_pre_g, w_router_t, b_router, band_cur, band_prev,
     x1_ref, h2_ref, eidx_ref, gate_ref, rank_ref, counts_ref,
     cx_buf, v_buf, u_buf, tmp_buf, tmp2_buf, shift_buf, carry_ref) = refs[n_in:]
    i = pl.program_id(0)
    t = i % steps_per_seq
    tm, d = x1_ref.shape

    @pl.when(t == 0)
    def _():
        cx_buf[0:SC_HALO, :] = jnp.zeros((SC_HALO, d), F32)
        v_buf[0:CF_HALO, :] = jnp.zeros((CF_HALO, d), F32)
        u_buf[...] = jnp.zeros((POOL_HALO, d), F32)

    @pl.when(i == 0)
    def _():
        carry_ref[...] = jnp.zeros(carry_ref.shape, F32)

    if has_comb:
        xp_ref, y0, y1, y2, y3, gate_prev, mod_prev, post_g_prev = x_in
        x = _combine(xp_ref, (y0, y1, y2, y3), gate_prev, mod_prev, post_g_prev)
    else:
        x = x_in[0][...]
    shift1, scale1, gate1 = mod_ref[0, 0:1, :], mod_ref[0, 1:2, :], mod_ref[0, 2:3, :]
    shift2, scale2 = mod_ref[0, 3:4, :], mod_ref[0, 4:5, :]

    hb = (_rms(x, pre_g[...]) * (1.0 + scale1) + shift1).astype(BF16)

    def proj(j):
        return jnp.dot(hb, w_in[:, j * d:(j + 1) * d], preferred_element_type=F32)

    v_buf[CF_HALO:CF_HALO + tm, :] = proj(3) * _sigmoid(proj(4))
    side = {}

    def pool_in():
        side["u"] = proj(5)

    def branch_a():
        cx_buf[SC_HALO:SC_HALO + tm, :] = proj(1) * proj(2)
        _causal_conv(cx_buf, sc_conv, SC_WIDTH, SC_HALO, tmp2_buf, shift_buf)
        cx_buf[0:SC_HALO, :] = cx_buf[tm:tm + SC_HALO, :]
        conv_a = (proj(0) * tmp2_buf[...]).astype(BF16)
        side["merged"] = _sigmoid(proj(6)) * jnp.dot(conv_a, sc_out[...], preferred_element_type=F32)

    def gate_b():
        side["gate_b"] = _sigmoid(proj(7))

    def gate_c():
        side["gate_c"] = _sigmoid(proj(8))

    _causal_conv(v_buf, cf_conv, CF_WIDTH, CF_HALO, tmp_buf, shift_buf, cf_conv_b,
                 before_chunk=(pool_in, branch_a, gate_b, gate_c))
    v_buf[0:CF_HALO, :] = v_buf[tm:tm + CF_HALO, :]
    v = tmp_buf[...]
    mu = jnp.mean(v, axis=-1, keepdims=True)
    vc = v - mu
    var = jnp.mean(vc * vc, axis=-1, keepdims=True)
    vn = vc * lax.rsqrt(var + LN_EPS) * cf_ln_g[...] + cf_ln_b[...]
    vn = vn * _sigmoid(vn)
    y_b = jnp.dot(vn.astype(BF16), cf_out[...], preferred_element_type=F32) + cf_out_b[...]
    merged = side["merged"] + side["gate_b"] * y_b

    u = side["u"]
    pos = (t * tm + 1 + lax.broadcasted_iota(jnp.int32, (tm, 1), 0)).astype(F32)
    group = d // len(POOL_WINDOWS)

    def split(a):
        hi = a.astype(BF16)
        return jnp.concatenate([hi, (a - hi.astype(F32)).astype(BF16)], axis=-1)

    y_c_parts = []
    for g, w in enumerate(POOL_WINDOWS):
        cols = slice(g * group, (g + 1) * group)
        s2 = jnp.dot(band_cur[g], split(u[:, cols]), preferred_element_type=F32)
        h2s = jnp.dot(band_prev[g], split(u_buf[:, cols]), preferred_element_type=F32)
        s = s2[:, :group] + s2[:, group:]
        s = jnp.concatenate([s[:POOL_HALO] + h2s[:, :group] + h2s[:, group:], s[POOL_HALO:]], axis=0)
        p = s * (1.0 / jnp.minimum(pos, float(w))) - u[:, cols]
        y_c_parts.append(jnp.dot(p.astype(BF16), pool_w[g], preferred_element_type=F32))
    u_buf[...] = u[tm - POOL_HALO:, :]
    y_c = jnp.concatenate(y_c_parts, axis=-1) * pool_scale[...]
    merged = merged + side["gate_c"] * y_c

    y = jnp.dot(merged.astype(BF16), w_o[...], preferred_element_type=F32)
    x1 = x + gate1 * _rms(y, post_g[...])
    x1_ref[...] = x1

    h2 = _rms(x1, moe_pre_g[...]) * (1.0 + scale2) + shift2
    _store_row_tiled(h2_ref, 0, h2)
    h2_hi = h2.astype(BF16)
    h2_split = jnp.concatenate([h2_hi, (h2 - h2_hi.astype(F32)).astype(BF16)], axis=0)
    cross = lax.dot_general(w_router_t[...], h2_split, (((1,), (1,)), ((), ())),
                            preferred_element_type=F32)
    logits = (cross[:N_EXPERTS, :tm] + cross[N_EXPERTS:, :tm] + cross[:N_EXPERTS, tm:]
              + b_router[...])
    eio = lax.broadcasted_iota(jnp.int32, (N_EXPERTS, tm), 0)
    neg_inf = jnp.float32(-jnp.inf)
    vals, idxs = [], []
    for _ in range(TOP_K):
        m = jnp.max(logits, axis=0, keepdims=True)
        idx = jnp.min(jnp.where(logits == m, eio, N_EXPERTS), axis=0, keepdims=True)
        vals.append(m)
        idxs.append(idx)
        logits = jnp.where(eio == idx, neg_inf, logits)
    exps = [jnp.exp(v - vals[0]) for v in vals]
    denom = exps[0] + exps[1] + exps[2] + exps[3]
    onehot = jnp.zeros((N_EXPERTS, tm), F32)
    for idx in idxs:
        onehot = onehot + (eio == idx).astype(F32)
    tri = (lax.broadcasted_iota(jnp.int32, (tm, tm), 0)
           < lax.broadcasted_iota(jnp.int32, (tm, tm), 1)).astype(BF16)
    carry = carry_ref[:, 0:1]
    before = jnp.dot(onehot.astype(BF16), tri, preferred_element_type=F32) + carry
    for k in range(TOP_K):
        eidx_ref[k:k + 1, :] = idxs[k]
        gate_ref[k:k + 1, :] = exps[k] / denom
        rank_ref[k:k + 1, :] = jnp.sum(jnp.where(eio == idxs[k], before, 0.0), axis=0,
                                       keepdims=True).astype(jnp.int32)
    carry = carry + jnp.sum(onehot, axis=1, keepdims=True)
    carry_ref[...] = jnp.broadcast_to(carry, carry_ref.shape)
    counts_ref[...] = jnp.broadcast_to(carry, counts_ref.shape).astype(jnp.int32)


def _split_bf16(a):
    hi = a.astype(BF16)
    return jnp.concatenate([hi, (a - hi.astype(F32)).astype(BF16)], axis=0)


def _pool_bands():
    t = jnp.arange(TM)[:, None]
    j = jnp.arange(TM)[None, :]
    th = jnp.arange(POOL_HALO)[:, None]
    jh = jnp.arange(POOL_HALO)[None, :] - POOL_HALO
    cur = jnp.stack([(j <= t) & (j > t - w) for w in POOL_WINDOWS]).astype(BF16)
    prev = jnp.stack([jh > th - w for w in POOL_WINDOWS]).astype(BF16)
    return cur, prev


def _mix_call(x2d, comb, mod, p, seq_len):
    has_comb = comb is not None
    n, d = comb[0].shape if has_comb else x2d.shape
    steps_per_seq = seq_len // TM
    n_steps = n // TM
    row = lambda a: a.reshape(1, d)
    taps = lambda a: jnp.broadcast_to(a[:, None, :], (a.shape[0], SUBLANES, d))
    if has_comb:
        x1p, y4p, gate_p, mod_p, post_g_p = comb
        lead_args = (x1p, y4p, y4p, y4p, y4p, gate_p, mod_p, row(post_g_p))
        lead_specs = _combine_specs(n, d, steps_per_seq)
    else:
        lead_args = (x2d,)
        lead_specs = [pl.BlockSpec((TM, d), lambda i: (i, 0))]
    const_args = (
        row(p["pre_g"]), row(p["post_g"]), p["w_in"], taps(p["sc_conv"]), p["sc_out"], taps(p["cf_conv"]),
        row(p["cf_conv_b"]), row(p["cf_ln_g"]), row(p["cf_ln_b"]), p["cf_out"], row(p["cf_out_b"]),
        p["pool_w"], row(p["pool_scale"]), p["w_o"], row(p["moe_pre_g"]), p["w_router_t"],
        p["b_router"].reshape(N_EXPERTS, 1), *_pool_bands(),
    )
    in_specs = (lead_specs + [pl.BlockSpec((1, N_MOD, d), lambda i: (i // steps_per_seq, 0, 0))]
                + [_const_spec(a.shape) for a in const_args])
    tok_spec = pl.BlockSpec((TOP_K, TM), lambda i: (0, i))
    out_shape = (
        jax.ShapeDtypeStruct((n, d), F32),
        jax.ShapeDtypeStruct((n * SUBLANES, LANES), F32),
        jax.ShapeDtypeStruct((TOP_K, n), jnp.int32),
        jax.ShapeDtypeStruct((TOP_K, n), F32),
        jax.ShapeDtypeStruct((TOP_K, n), jnp.int32),
        jax.ShapeDtypeStruct((N_EXPERTS, LANES), jnp.int32),
    )
    out_specs = (
        pl.BlockSpec((TM, d), lambda i: (i, 0)),
        pl.BlockSpec((TM * SUBLANES, LANES), lambda i: (i, 0)),
        tok_spec, tok_spec, tok_spec,
        pl.BlockSpec((N_EXPERTS, LANES), lambda i: (0, 0)),
    )
    return pl.pallas_call(
        functools.partial(_mix_kernel, steps_per_seq=steps_per_seq, has_comb=has_comb),
        grid=(n_steps,),
        in_specs=in_specs,
        out_specs=out_specs,
        out_shape=out_shape,
        scratch_shapes=[
            pltpu.VMEM((SC_HALO + TM, d), F32),
            pltpu.VMEM((CF_HALO + TM, d), F32),
            pltpu.VMEM((POOL_HALO, d), F32),
            pltpu.VMEM((TM, d), F32),
            pltpu.VMEM((TM, d), F32),
            pltpu.VMEM((SUBLANES - 1, CF_HALO + TM - SUBLANES, COL_CHUNK), F32),
            pltpu.VMEM((N_EXPERTS, LANES), F32),
        ],
        compiler_params=pltpu.CompilerParams(dimension_semantics=("arbitrary",),
                                             vmem_limit_bytes=VMEM_LIMIT),
        name="mixer_router",
    )(*lead_args, mod, *const_args)


def _route_kernel(pstart_ref, pvalid_ref, pend_ref, eidx_ref, rank_ref, inv_ref,
                  dest_vmem, dest_smem, sem, *, n_chunks, n_asg):
    c = pl.program_id(0)
    n_pad = inv_ref.shape[0]
    chunk = ROUTE_ROWS * LANES

    def to_smem(s):
        return pltpu.make_async_copy(dest_vmem.at[s], dest_smem.at[s], sem.at[s])

    @pl.when(c == 0)
    def _():
        def pad_slot(p, carry):
            inv_ref[p] = n_asg + (p & (2 * BM - 1))
            return carry

        def pad_expert(e, carry):
            lax.fori_loop(pvalid_ref[e], pend_ref[e], pad_slot, 0)
            return carry

        lax.fori_loop(0, N_EXPERTS, pad_expert, 0)
        lax.fori_loop(pend_ref[N_EXPERTS - 1], n_pad, pad_slot, 0)

    @pl.when(c < n_chunks)
    def _():
        eidx = eidx_ref[...]
        dest = rank_ref[...]
        for e in range(N_EXPERTS):
            dest = dest + jnp.where(eidx == e, pstart_ref[e], 0)
        dest_vmem[c % 2] = dest
        to_smem(c % 2).start()

    @pl.when(c >= 1)
    def _():
        s = (c - 1) % 2
        to_smem(s).wait()
        base = (c - 1) * chunk

        def row(r, carry):
            for col in range(LANES):
                inv_ref[dest_smem[s, r, col]] = base + r * LANES + col
            return carry

        lax.fori_loop(0, ROUTE_ROWS, row, 0)


def _routing_tables(eidx, rank, counts):
    n = eidx.shape[1]
    n_asg = n * TOP_K
    n_pad = n_asg + N_EXPERTS * BM
    nb_max = n_pad // BM
    chunk = ROUTE_ROWS * LANES
    n_chunks = n_asg // chunk
    padded = (counts + BM - 1) // BM * BM
    pend = jnp.cumsum(padded).astype(jnp.int32)
    pstart = pend - padded
    n_blocks = pend[-1:] // BM
    blk = jnp.arange(nb_max, dtype=jnp.int32)
    block_e = jnp.minimum(jnp.sum((pend[None, :] <= blk[:, None] * BM).astype(jnp.int32), axis=1),
                          N_EXPERTS - 1)
    last_e = jnp.sum(jnp.where(blk == n_blocks[0] - 1, block_e, 0))
    block_e = jnp.where(blk < n_blocks[0], block_e, last_e).astype(jnp.int32)
    smem = pl.BlockSpec(memory_space=pltpu.SMEM)
    tile = pl.BlockSpec((ROUTE_ROWS, LANES), lambda c: (jnp.minimum(c, n_chunks - 1), 0))
    inv = pl.pallas_call(
        functools.partial(_route_kernel, n_chunks=n_chunks, n_asg=n_asg),
        grid=(n_chunks + 1,),
        in_specs=[smem, smem, smem, tile, tile],
        out_specs=smem,
        out_shape=jax.ShapeDtypeStruct((n_pad + LOOKAHEAD_BLOCKS * BM,), jnp.int32),
        scratch_shapes=[
            pltpu.VMEM((2, ROUTE_ROWS, LANES), jnp.int32),
            pltpu.SMEM((2, ROUTE_ROWS, LANES), jnp.int32),
            pltpu.SemaphoreType.DMA((2,)),
        ],
        compiler_params=pltpu.CompilerParams(dimension_semantics=("arbitrary",)),
        name="route_tables",
    )(pstart, pstart + counts, pend, eidx.reshape(n_asg // LANES, LANES),
      rank.reshape(n_asg // LANES, LANES))
    return inv, block_e, n_blocks


def _expert_kernel(be_ref, nb_ref, inv_hbm, h2_hbm, wgu_ref, bgu_ref, wd_ref, bd_ref, y4_hbm,
                   idx_smem, xbuf, ybuf, wgu_bf, wd_bf, isem, gsem, ssem, zsem,
                   *, nb_max, n_tok):
    i = pl.program_id(0)
    nb = nb_ref[0]
    s_cur = i % 3
    s_next = (i + 1) % 3
    s_prev = (i + 2) % 3
    q_ahead2 = (i + 2) % 5
    q_ahead3 = (i + 3) % 5
    q_prev = (i + 4) % 5
    rows = BM * SUBLANES
    d_ff = wd_bf.shape[0]
    n_asg = TOP_K * n_tok

    def idx_copy(blk, s):
        return pltpu.make_async_copy(inv_hbm.at[pl.ds(blk * BM, BM)], idx_smem.at[s], isem.at[s])

    def gather_copy(s_idx, g, r):
        tok = idx_smem[s_idx, r] & (n_tok - 1)
        return pltpu.make_async_copy(
            h2_hbm.at[pl.ds(pl.multiple_of(tok * SUBLANES, SUBLANES), SUBLANES)],
            xbuf.at[pl.ds(pl.multiple_of(g * rows + r * SUBLANES, SUBLANES), SUBLANES)],
            gsem.at[g])

    def scatter_copy(s_idx, s, r):
        dst = idx_smem[s_idx, r]
        return pltpu.make_async_copy(
            ybuf.at[pl.ds(pl.multiple_of(s * rows + r * SUBLANES, SUBLANES), SUBLANES)],
            y4_hbm.at[pl.ds(pl.multiple_of(dst * SUBLANES, SUBLANES), SUBLANES)],
            ssem.at[s])

    def wait_gather(g):
        pltpu.make_async_copy(h2_hbm.at[pl.ds(0, rows)], xbuf.at[pl.ds(g * rows, rows)],
                              gsem.at[g]).wait()

    def wait_scatter(s):
        pltpu.make_async_copy(ybuf.at[pl.ds(s * rows, rows)], y4_hbm.at[pl.ds(0, rows)],
                              ssem.at[s]).wait()

    @pl.when(i == 0)
    def _():
        ybuf[...] = jnp.zeros(ybuf.shape, F32)
        spare = pltpu.make_async_copy(
            ybuf.at[pl.ds(0, DUMP_BLOCKS * rows)],
            y4_hbm.at[pl.ds(n_asg * SUBLANES, DUMP_BLOCKS * rows)], zsem)
        spare.start()
        spare.wait()

        def fill(r, carry):
            idx_smem[4, r] = n_asg + 2 * BM + r
            return carry
        lax.fori_loop(0, BM, fill, 0)
        for b in range(2):
            idx_copy(b, b).start()
            idx_copy(b, b).wait()

            def body(r, carry, b=b):
                gather_copy(b, b, r).start()
                return carry
            lax.fori_loop(0, BM, body, 0)
        idx_copy(2, 2).start()

    prev_e = be_ref[jnp.maximum(i - 1, 0)]

    @pl.when((i < nb) & ((i == 0) | (be_ref[i] != prev_e)))
    def _():
        wgu_bf[...] = wgu_ref[0, 0].astype(BF16)
        wd_bf[...] = wd_ref[0, 0].astype(BF16)

    @pl.when((i < nb) & (i >= 2))
    def _():
        wait_scatter(s_cur)

    @pl.when(i < nb)
    def _():
        idx_copy(i + 3, q_ahead3).start()
        wait_gather(s_cur)
        idx_copy(i + 2, q_ahead2).wait()
        n_chunk = d_ff // MXU_DEPTH
        halves = MXU_DEPTH // LANES
        per_gather = -(-BM // (n_chunk - 1))
        per_scatter = BM // (2 * n_chunk)
        act_base = DUMP_BLOCKS * rows
        bgu = bgu_ref[0, 0]

        def scatter_group(g):
            for r in range(g * per_scatter, (g + 1) * per_scatter):
                scatter_copy(q_prev, s_prev, r).start(priority=r % 2)

        for c in range(n_chunk):
            xb = _load_row_tiled(xbuf, s_cur * rows, BM).astype(BF16)
            gcols = slice(c * MXU_DEPTH, (c + 1) * MXU_DEPTH)
            ucols = slice(d_ff + c * MXU_DEPTH, d_ff + (c + 1) * MXU_DEPTH)
            gt = jnp.dot(xb, wgu_bf[:, gcols], preferred_element_type=F32) + bgu[:, gcols]
            up = jnp.dot(xb, wgu_bf[:, ucols], preferred_element_type=F32) + bgu[:, ucols]
            gt = jnp.minimum(gt, SWIGLU_LIMIT)
            up = jnp.clip(up, -SWIGLU_LIMIT, SWIGLU_LIMIT)
            act_c = (up + 1.0) * gt * _sigmoid(gt * SWIGLU_ALPHA)
            if c < n_chunk - 1:
                for r in range(c * per_gather, min((c + 1) * per_gather, BM)):
                    gather_copy(q_ahead2, s_prev, r).start(priority=r % 2)
            scatter_group(c)
            for h in range(halves):
                ybuf[pl.ds(act_base + (c * halves + h) * BM, BM), :] = act_c[:, h * LANES:(h + 1) * LANES]
        act = jnp.concatenate(
            [ybuf[pl.ds(act_base + j * BM, BM), :] for j in range(n_chunk * halves)],
            axis=-1).astype(BF16)
        for c in range(n_chunk):
            cols = slice(c * MXU_DEPTH, (c + 1) * MXU_DEPTH)
            yc = jnp.dot(act, wd_bf[:, cols], preferred_element_type=F32) + bd_ref[0, 0][:, cols]
            scatter_group(n_chunk + c)
            for h in range(halves):
                ybuf[pl.ds(s_cur * rows + c * halves + h, BM, stride=SUBLANES), :] = (
                    yc[:, h * LANES:(h + 1) * LANES])

    @pl.when(i == nb)
    def _():
        wait_gather(s_cur)
        wait_gather(s_next)
        idx_copy(i + 2, q_ahead2).wait()

        @pl.when(i >= 2)
        def _():
            wait_scatter(s_cur)

        def body(r, carry):
            scatter_copy(q_prev, s_prev, r).start()
            return carry
        lax.fori_loop(0, BM, body, 0)
        wait_scatter(s_next)
        wait_scatter(s_prev)


def _expert_call(layer, block_e, n_blocks, inv, h2, w_gu, b_gu, w_down, b_down):
    n_tok = h2.shape[0] // SUBLANES
    depth, n_exp, d, two_ff = w_gu.shape
    d_ff = two_ff // 2
    nb_max = inv.shape[0] // BM - LOOKAHEAD_BLOCKS
    rows = BM * SUBLANES
    grid_spec = pltpu.PrefetchScalarGridSpec(
        num_scalar_prefetch=2,
        grid=(nb_max,),
        in_specs=[
            pl.BlockSpec(memory_space=pl.ANY),
            pl.BlockSpec(memory_space=pl.ANY),
            pl.BlockSpec((1, 1, d, two_ff), lambda i, be, nb: (layer, be[i], 0, 0)),
            pl.BlockSpec((1, 1, 1, two_ff), lambda i, be, nb: (layer, be[i], 0, 0)),
            pl.BlockSpec((1, 1, d_ff, d), lambda i, be, nb: (layer, be[i], 0, 0)),
            pl.BlockSpec((1, 1, 1, d), lambda i, be, nb: (layer, be[i], 0, 0)),
        ],
        out_specs=pl.BlockSpec(memory_space=pl.ANY),
        scratch_shapes=[
            pltpu.SMEM((5, BM), jnp.int32),
            pltpu.VMEM((3 * rows, LANES), F32),
            pltpu.VMEM(((DUMP_BLOCKS + 1) * rows, LANES), F32),
            pltpu.VMEM((d, two_ff), BF16),
            pltpu.VMEM((d_ff, d), BF16),
            pltpu.SemaphoreType.DMA((5,)),
            pltpu.SemaphoreType.DMA((3,)),
            pltpu.SemaphoreType.DMA((3,)),
            pltpu.SemaphoreType.DMA(()),
        ],
    )
    return pl.pallas_call(
        functools.partial(_expert_kernel, nb_max=nb_max, n_tok=n_tok),
        grid_spec=grid_spec,
        out_shape=jax.ShapeDtypeStruct(((TOP_K * n_tok + DUMP_BLOCKS * BM) * SUBLANES, LANES), F32),
        compiler_params=pltpu.CompilerParams(dimension_semantics=("arbitrary",),
                                             vmem_limit_bytes=VMEM_LIMIT),
        name="experts",
    )(block_e, n_blocks, inv, h2, w_gu, b_gu.reshape(depth, n_exp, 1, two_ff), w_down,
      b_down.reshape(depth, n_exp, 1, d))


def kernel(x, c, ada_w, ada_b, mix_pre_g, mix_post_g, w_in, sc_conv, sc_out, cf_conv, cf_conv_b, cf_ln_g, cf_ln_b, cf_out, cf_out_b, pool_w, pool_scale, w_o, moe_pre_g, moe_post_g, w_router, b_router, w_gu, b_gu, w_down, b_down):
    bsz, seq_len, d = x.shape
    depth = ada_w.shape[0]
    n_tok = bsz * seq_len
    assert d == SUBLANES * LANES and seq_len % TM == 0 and d % COL_CHUNK == 0 and TM % ROW_CHUNK == 0
    assert (n_tok * TOP_K) % (ROUTE_ROWS * LANES) == 0 and (n_tok * TOP_K) % BM == 0
    assert n_tok >= 2 * BM and n_tok & (n_tok - 1) == 0 and BM & (BM - 1) == 0
    mods = _ada_call(c, ada_w, ada_b)
    xs = x.reshape(n_tok, d)
    comb = None
    for l in range(depth):
        p = dict(
            pre_g=mix_pre_g[l], post_g=mix_post_g[l], w_in=w_in[l].astype(BF16), sc_conv=sc_conv[l],
            sc_out=sc_out[l].astype(BF16), cf_conv=cf_conv[l], cf_conv_b=cf_conv_b[l], cf_ln_g=cf_ln_g[l],
            cf_ln_b=cf_ln_b[l], cf_out=cf_out[l].astype(BF16), cf_out_b=cf_out_b[l],
            pool_w=pool_w[l].astype(BF16), pool_scale=pool_scale[l], w_o=w_o[l].astype(BF16),
            moe_pre_g=moe_pre_g[l], w_router_t=_split_bf16(w_router[l].T), b_router=b_router[l],
        )
        x1, h2, eidx, gate, rank, counts = _mix_call(xs if comb is None else None, comb, mods[l], p, seq_len)
        inv, block_e, n_blocks = _routing_tables(eidx, rank, counts[:, 0])
        y4 = _expert_call(l, block_e, n_blocks, inv, h2, w_gu, b_gu, w_down, b_down)
        comb = (x1, y4, gate.T, mods[l], moe_post_g[l])
    return _final_call(comb, seq_len).reshape(bsz, seq_len, d)
```

```python
import functools

import jax
import jax.numpy as jnp
from jax import lax
from jax.experimental import pallas as pl
from jax.experimental.pallas import tpu as pltpu

F32 = jnp.float32
BF16 = jnp.bfloat16

N_EXPERTS = 32
TOP_K = 4
N_MOD = 6
SC_WIDTH = 3
CF_WIDTH = 31
POOL_WINDOWS = (2, 4, 8, 16)
SWIGLU_LIMIT = 7.0
SWIGLU_ALPHA = 1.702
NORM_EPS = 1e-6
LN_EPS = 1e-5

SUBLANES = 8
LANES = 128
MXU_DEPTH = 256
TM = 256
BM = 256
SC_HALO = 8
CF_HALO = 32
POOL_HALO = 16
ROW_CHUNK = 32
COL_CHUNK = 256
ROUTE_ROWS = 64
DUMP_BLOCKS = 3
LOOKAHEAD_BLOCKS = 2
VMEM_LIMIT = 60 * 1024 * 1024


def _rms(x, g):
    return x * lax.rsqrt(jnp.mean(x * x, axis=-1, keepdims=True) + NORM_EPS) * g


def _sigmoid(x):
    return 0.5 * jnp.tanh(0.5 * x) + 0.5


def _const_spec(shape):
    nd = len(shape)
    return pl.BlockSpec(shape, lambda *_: (0,) * nd, pipeline_mode=pl.Buffered(1))


class _CopyGroup:
    def __init__(self, copies):
        self.copies = copies

    def start(self):
        for c in self.copies:
            c.start()

    def wait(self):
        for c in self.copies:
            c.wait()


def _load_row_tiled(ref, base, rows):
    return jnp.concatenate(
        [ref[pl.ds(base + s, rows, stride=SUBLANES), :] for s in range(SUBLANES)], axis=-1)


def _store_row_tiled(ref, base, value):
    rows = value.shape[0]
    for s in range(SUBLANES):
        ref[pl.ds(base + s, rows, stride=SUBLANES), :] = value[:, s * LANES:(s + 1) * LANES]


def _ada_kernel(c_ref, w_ref, b_ref, o_ref):
    c = c_ref[...]
    c_act = c * _sigmoid(c)
    o_ref[0, 0] = jnp.dot(c_act, w_ref[0], preferred_element_type=F32,
                          precision=lax.Precision.HIGHEST) + b_ref[0, 0]


def _ada_call(c, ada_w, ada_b):
    depth, d, _ = ada_w.shape
    bsz = c.shape[0]
    ada_b4 = ada_b.reshape(depth, N_MOD, 1, d)
    out = pl.pallas_call(
        _ada_kernel,
        grid=(depth, N_MOD),
        in_specs=[
            pl.BlockSpec((bsz, d), lambda l, j: (0, 0)),
            pl.BlockSpec((1, d, d), lambda l, j: (l, 0, j)),
            pl.BlockSpec((1, 1, 1, d), lambda l, j: (l, j, 0, 0)),
        ],
        out_specs=pl.BlockSpec((1, 1, bsz, d), lambda l, j: (l, j, 0, 0)),
        out_shape=jax.ShapeDtypeStruct((depth, N_MOD, bsz, d), F32),
        name="ada_mod",
    )(c, ada_w, ada_b4)
    return jnp.transpose(out, (0, 2, 1, 3))


def _combine(x1_ref, y4_refs, gate_ref, mod_ref, post_g):
    tm = x1_ref.shape[0]
    gate = gate_ref[...]
    y = None
    for k, y_ref in enumerate(y4_refs):
        term = gate[:, k:k + 1] * _load_row_tiled(y_ref, 0, tm)
        y = term if y is None else y + term
    return x1_ref[...] + mod_ref[0, 5:6, :] * _rms(y, post_g[...])


def _combine_specs(n, d, steps_per_seq):
    blocks_per_plane = n // TM
    specs = [pl.BlockSpec((TM, d), lambda i: (i, 0))]
    for k in range(TOP_K):
        specs.append(pl.BlockSpec((TM * SUBLANES, LANES),
                                  lambda i, k=k: (k * blocks_per_plane + i, 0)))
    specs += [
        pl.BlockSpec((TM, TOP_K), lambda i: (i, 0)),
        pl.BlockSpec((1, N_MOD, d), lambda i: (i // steps_per_seq, 0, 0)),
        pl.BlockSpec((1, d), lambda i: (0, 0)),
    ]
    return specs


def _final_kernel(x1_ref, y0, y1, y2, y3, gate_ref, mod_ref, post_g, out_ref):
    out_ref[...] = _combine(x1_ref, (y0, y1, y2, y3), gate_ref, mod_ref, post_g)


def _final_call(comb, seq_len):
    x1, y4, gate_nk, mod, post_g = comb
    n, d = x1.shape
    return pl.pallas_call(
        _final_kernel,
        grid=(n // TM,),
        in_specs=_combine_specs(n, d, seq_len // TM),
        out_specs=pl.BlockSpec((TM, d), lambda i: (i, 0)),
        out_shape=jax.ShapeDtypeStruct((n, d), F32),
        compiler_params=pltpu.CompilerParams(dimension_semantics=("arbitrary",),
                                             vmem_limit_bytes=VMEM_LIMIT),
        name="final_combine",
    )(x1, y4, y4, y4, y4, gate_nk, mod, post_g.reshape(1, d))


def _causal_conv(buf_ref, w_ref, width, halo, out_ref, shift_ref, bias_ref=None, before_chunk=()):
    tm, d = out_ref.shape
    n_rows = buf_ref.shape[0] - SUBLANES
    offsets = [halo - (width - 1) + k for k in range(width)]
    for ci, c0 in enumerate(range(0, d, COL_CHUNK)):
        if ci < len(before_chunk):
            before_chunk[ci]()
        cols = slice(c0, c0 + COL_CHUNK)
        for q in sorted({o % SUBLANES for o in offsets} - {0}):
            shift_ref[q - 1, 0:n_rows, :] = buf_ref[q:q + n_rows, cols]
        for r0 in range(0, tm, ROW_CHUNK):
            accs = [None] * (ROW_CHUNK // SUBLANES)
            for k, o in enumerate(offsets):
                q = o % SUBLANES
                w8 = w_ref[k, :, cols]
                for j in range(len(accs)):
                    start = o - q + r0 + j * SUBLANES
                    if q == 0:
                        rows = buf_ref[start:start + SUBLANES, cols]
                    else:
                        rows = shift_ref[q - 1, start:start + SUBLANES, :]
                    accs[j] = w8 * rows if accs[j] is None else accs[j] + w8 * rows
            for j, acc in enumerate(accs):
                if bias_ref is not None:
                    acc = acc + bias_ref[:, cols]
                out_ref[r0 + j * SUBLANES:r0 + (j + 1) * SUBLANES, cols] = acc


def _mix_kernel(*refs, steps_per_seq, has_comb):
    n_in = 8 if has_comb else 1
    x_in = refs[:n_in]
    (mod_ref, pre_g, post_g, w_in, sc_conv, sc_out, cf_conv, cf_conv_b, cf_ln_g, cf_ln_b, cf_out,
     cf_out_b, pool_w, pool_scale, w_o, moe_pre_g, w_router_t, b_router, band_cur, band_prev,
     x1_ref, h2_ref, eidx_ref, gate_ref, rank_ref, counts_ref,
     cx_buf, v_buf, u_buf, tmp_buf, tmp2_buf, shift_buf, carry_ref) = refs[n_in:]
    i = pl.program_id(0)
    t = i % steps_per_seq
    tm, d = x1_ref.shape

    @pl.when(t == 0)
    def _():
        cx_buf[0:SC_HALO, :] = jnp.zeros((SC_HALO, d), F32)
        v_buf[0:CF_HALO, :] = jnp.zeros((CF_HALO, d), F32)
        u_buf[...] = jnp.zeros((POOL_HALO, d), F32)

    @pl.when(i == 0)
    def _():
        carry_ref[...] = jnp.zeros(carry_ref.shape, F32)

    if has_comb:
        xp_ref, y0, y1, y2, y3, gate_prev, mod_prev, post_g_prev = x_in
        x = _combine(xp_ref, (y0, y1, y2, y3), gate_prev, mod_prev, post_g_prev)
    else:
        x = x_in[0][...]
    shift1, scale1, gate1 = mod_ref[0, 0:1, :], mod_ref[0, 1:2, :], mod_ref[0, 2:3, :]
    shift2, scale2 = mod_ref[0, 3:4, :], mod_ref[0, 4:5, :]

    hb = (_rms(x, pre_g[...]) * (1.0 + scale1) + shift1).astype(BF16)

    def proj(j):
        return jnp.dot(hb, w_in[:, j * d:(j + 1) * d], preferred_element_type=F32)

    v_buf[CF_HALO:CF_HALO + tm, :] = proj(3) * _sigmoid(proj(4))
    side = {}

    def pool_in():
        side["u"] = proj(5)

    def branch_a():
        cx_buf[SC_HALO:SC_HALO + tm, :] = proj(1) * proj(2)
        _causal_conv(cx_buf, sc_conv, SC_WIDTH, SC_HALO, tmp2_buf, shift_buf)
        cx_buf[0:SC_HALO, :] = cx_buf[tm:tm + SC_HALO, :]
        conv_a = (proj(0) * tmp2_buf[...]).astype(BF16)
        side["merged"] = _sigmoid(proj(6)) * jnp.dot(conv_a, sc_out[...], preferred_element_type=F32)

    def gate_b():
        side["gate_b"] = _sigmoid(proj(7))

    def gate_c():
        side["gate_c"] = _sigmoid(proj(8))

    _causal_conv(v_buf, cf_conv, CF_WIDTH, CF_HALO, tmp_buf, shift_buf, cf_conv_b,
                 before_chunk=(pool_in, branch_a, gate_b, gate_c))
    v_buf[0:CF_HALO, :] = v_buf[tm:tm + CF_HALO, :]
    v = tmp_buf[...]
    mu = jnp.mean(v, axis=-1, keepdims=True)
    vc = v - mu
    var = jnp.mean(vc * vc, axis=-1, keepdims=True)
    vn = vc * lax.rsqrt(var + LN_EPS) * cf_ln_g[...] + cf_ln_b[...]
    vn = vn * _sigmoid(vn)
    y_b = jnp.dot(vn.astype(BF16), cf_out[...], preferred_element_type=F32) + cf_out_b[...]
    merged = side["merged"] + side["gate_b"] * y_b

    u = side["u"]
    pos = (t * tm + 1 + lax.broadcasted_iota(jnp.int32, (tm, 1), 0)).astype(F32)
    group = d // len(POOL_WINDOWS)

    def split(a):
        hi = a.astype(BF16)
        return jnp.concatenate([hi, (a - hi.astype(F32)).astype(BF16)], axis=-1)

    y_c_parts = []
    for g, w in enumerate(POOL_WINDOWS):
        cols = slice(g * group, (g + 1) * group)
        s2 = jnp.dot(band_cur[g], split(u[:, cols]), preferred_element_type=F32)
        h2s = jnp.dot(band_prev[g], split(u_buf[:, cols]), preferred_element_type=F32)
        s = s2[:, :group] + s2[:, group:]
        s = jnp.concatenate([s[:POOL_HALO] + h2s[:, :group] + h2s[:, group:], s[POOL_HALO:]], axis=0)
        p = s * (1.0 / jnp.minimum(pos, float(w))) - u[:, cols]
        y_c_parts.append(jnp.dot(p.astype(BF16), pool_w[g], preferred_element_type=F32))
    u_buf[...] = u[tm - POOL_HALO:, :]
    y_c = jnp.concatenate(y_c_parts, axis=-1) * pool_scale[...]
    merged = merged + side["gate_c"] * y_c

    y = jnp.dot(merged.astype(BF16), w_o[...], preferred_element_type=F32)
    x1 = x + gate1 * _rms(y, post_g[...])
    x1_ref[...] = x1

    h2 = _rms(x1, moe_pre_g[...]) * (1.0 + scale2) + shift2
    _store_row_tiled(h2_ref, 0, h2)
    h2_hi = h2.astype(BF16)
    h2_split = jnp.concatenate([h2_hi, (h2 - h2_hi.astype(F32)).astype(BF16)], axis=0)
    cross = lax.dot_general(w_router_t[...], h2_split, (((1,), (1,)), ((), ())),
                            preferred_element_type=F32)
    logits = (cross[:N_EXPERTS, :tm] + cross[N_EXPERTS:, :tm] + cross[:N_EXPERTS, tm:]
              + b_router[...])
    eio = lax.broadcasted_iota(jnp.int32, (N_EXPERTS, tm), 0)
    neg_inf = jnp.float32(-jnp.inf)
    vals, idxs = [], []
    for _ in range(TOP_K):
        m = jnp.max(logits, axis=0, keepdims=True)
        idx = jnp.min(jnp.where(logits == m, eio, N_EXPERTS), axis=0, keepdims=True)
        vals.append(m)
        idxs.append(idx)
        logits = jnp.where(eio == idx, neg_inf, logits)
    exps = [jnp.exp(v - vals[0]) for v in vals]
    denom = exps[0] + exps[1] + exps[2] + exps[3]
    onehot = jnp.zeros((N_EXPERTS, tm), F32)
    for idx in idxs:
        onehot = onehot + (eio == idx).astype(F32)
    tri = (lax.broadcasted_iota(jnp.int32, (tm, tm), 0)
           < lax.broadcasted_iota(jnp.int32, (tm, tm), 1)).astype(BF16)
    carry = carry_ref[:, 0:1]
    before = jnp.dot(onehot.astype(BF16), tri, preferred_element_type=F32) + carry
    for k in range(TOP_K):
        eidx_ref[k:k + 1, :] = idxs[k]
        gate_ref[k:k + 1, :] = exps[k] / denom
        rank_ref[k:k + 1, :] = jnp.sum(jnp.where(eio == idxs[k], before, 0.0), axis=0,
                                       keepdims=True).astype(jnp.int32)
    carry = carry + jnp.sum(onehot, axis=1, keepdims=True)
    carry_ref[...] = jnp.broadcast_to(carry, carry_ref.shape)
    counts_ref[...] = jnp.broadcast_to(carry, counts_ref.shape).astype(jnp.int32)


def _split_bf16(a):
    hi = a.astype(BF16)
    return jnp.concatenate([hi, (a - hi.astype(F32)).astype(BF16)], axis=0)


def _pool_bands():
    t = jnp.arange(TM)[:, None]
    j = jnp.arange(TM)[None, :]
    th = jnp.arange(POOL_HALO)[:, None]
    jh = jnp.arange(POOL_HALO)[None, :] - POOL_HALO
    cur = jnp.stack([(j <= t) & (j > t - w) for w in POOL_WINDOWS]).astype(BF16)
    prev = jnp.stack([jh > th - w for w in POOL_WINDOWS]).astype(BF16)
    return cur, prev


def _mix_call(x2d, comb, mod, p, seq_len):
    has_comb = comb is not None
    n, d = comb[0].shape if has_comb else x2d.shape
    steps_per_seq = seq_len // TM
    n_steps = n // TM
    row = lambda a: a.reshape(1, d)
    taps = lambda a: jnp.broadcast_to(a[:, None, :], (a.shape[0], SUBLANES, d))
    if has_comb:
        x1p, y4p, gate_p, mod_p, post_g_p = comb
        lead_args = (x1p, y4p, y4p, y4p, y4p, gate_p, mod_p, row(post_g_p))
        lead_specs = _combine_specs(n, d, steps_per_seq)
    else:
        lead_args = (x2d,)
        lead_specs = [pl.BlockSpec((TM, d), lambda i: (i, 0))]
    const_args = (
        row(p["pre_g"]), row(p["post_g"]), p["w_in"], taps(p["sc_conv"]), p["sc_out"], taps(p["cf_conv"]),
        row(p["cf_conv_b"]), row(p["cf_ln_g"]), row(p["cf_ln_b"]), p["cf_out"], row(p["cf_out_b"]),
        p["pool_w"], row(p["pool_scale"]), p["w_o"], row(p["moe_pre_g"]), p["w_router_t"],
        p["b_router"].reshape(N_EXPERTS, 1), *_pool_bands(),
    )
    in_specs = (lead_specs + [pl.BlockSpec((1, N_MOD, d), lambda i: (i // steps_per_seq, 0, 0))]
                + [_const_spec(a.shape) for a in const_args])
    tok_spec = pl.BlockSpec((TOP_K, TM), lambda i: (0, i))
    out_shape = (
        jax.ShapeDtypeStruct((n, d), F32),
        jax.ShapeDtypeStruct((n * SUBLANES, LANES), F32),
        jax.ShapeDtypeStruct((TOP_K, n), jnp.int32),
        jax.ShapeDtypeStruct((TOP_K, n), F32),
        jax.ShapeDtypeStruct((TOP_K, n), jnp.int32),
        jax.ShapeDtypeStruct((N_EXPERTS, LANES), jnp.int32),
    )
    out_specs = (
        pl.BlockSpec((TM, d), lambda i: (i, 0)),
        pl.BlockSpec((TM * SUBLANES, LANES), lambda i: (i, 0)),
        tok_spec, tok_spec, tok_spec,
        pl.BlockSpec((N_EXPERTS, LANES), lambda i: (0, 0)),
    )
    return pl.pallas_call(
        functools.partial(_mix_kernel, steps_per_seq=steps_per_seq, has_comb=has_comb),
        grid=(n_steps,),
        in_specs=in_specs,
        out_specs=out_specs,
        out_shape=out_shape,
        scratch_shapes=[
            pltpu.VMEM((SC_HALO + TM, d), F32),
            pltpu.VMEM((CF_HALO + TM, d), F32),
            pltpu.VMEM((POOL_HALO, d), F32),
            pltpu.VMEM((TM, d), F32),
            pltpu.VMEM((TM, d), F32),
            pltpu.VMEM((SUBLANES - 1, CF_HALO + TM - SUBLANES, COL_CHUNK), F32),
            pltpu.VMEM((N_EXPERTS, LANES), F32),
        ],
        compiler_params=pltpu.CompilerParams(dimension_semantics=("arbitrary",),
                                             vmem_limit_bytes=VMEM_LIMIT),
        name="mixer_router",
    )(*lead_args, mod, *const_args)


def _route_kernel(pstart_ref, pvalid_ref, pend_ref, eidx_ref, rank_ref, inv_ref,
                  dest_vmem, dest_smem, sem, *, n_chunks, n_asg):
    c = pl.program_id(0)
    n_pad = inv_ref.shape[0]
    chunk = ROUTE_ROWS * LANES

    def to_smem(s):
        return pltpu.make_async_copy(dest_vmem.at[s], dest_smem.at[s], sem.at[s])

    @pl.when(c == 0)
    def _():
        def pad_slot(p, carry):
            inv_ref[p] = n_asg + (p & (2 * BM - 1))
            return carry

        def pad_expert(e, carry):
            lax.fori_loop(pvalid_ref[e], pend_ref[e], pad_slot, 0)
            return carry

        lax.fori_loop(0, N_EXPERTS, pad_expert, 0)
        lax.fori_loop(pend_ref[N_EXPERTS - 1], n_pad, pad_slot, 0)

    @pl.when(c < n_chunks)
    def _():
        eidx = eidx_ref[...]
        dest = rank_ref[...]
        for e in range(N_EXPERTS):
            dest = dest + jnp.where(eidx == e, pstart_ref[e], 0)
        dest_vmem[c % 2] = dest
        to_smem(c % 2).start()

    @pl.when(c >= 1)
    def _():
        s = (c - 1) % 2
        to_smem(s).wait()
        base = (c - 1) * chunk

        def row(r, carry):
            for col in range(LANES):
                inv_ref[dest_smem[s, r, col]] = base + r * LANES + col
            return carry

        lax.fori_loop(0, ROUTE_ROWS, row, 0)


def _routing_tables(eidx, rank, counts):
    n = eidx.shape[1]
    n_asg = n * TOP_K
    n_pad = n_asg + N_EXPERTS * BM
    nb_max = n_pad // BM
    chunk = ROUTE_ROWS * LANES
    n_chunks = n_asg // chunk
    padded = (counts + BM - 1) // BM * BM
    pend = jnp.cumsum(padded).astype(jnp.int32)
    pstart = pend - padded
    n_blocks = pend[-1:] // BM
    blk = jnp.arange(nb_max, dtype=jnp.int32)
    block_e = jnp.minimum(jnp.sum((pend[None, :] <= blk[:, None] * BM).astype(jnp.int32), axis=1),
                          N_EXPERTS - 1)
    last_e = jnp.sum(jnp.where(blk == n_blocks[0] - 1, block_e, 0))
    block_e = jnp.where(blk < n_blocks[0], block_e, last_e).astype(jnp.int32)
    used = blk < n_blocks[0]
    first = jnp.concatenate([jnp.ones((1,), bool), block_e[1:] != block_e[:-1]]) & used
    parity = (jnp.cumsum(first.astype(jnp.int32)) - 1) % 2
    later = (block_e[None, :] > block_e[:, None]) & used[None, :]
    next_e = jnp.min(jnp.where(later, block_e[None, :], N_EXPERTS), axis=1)
    next_e = jnp.where(next_e == N_EXPERTS, -1, next_e)
    block_meta = jnp.stack([block_e, first.astype(jnp.int32), parity, next_e]).astype(jnp.int32)
    smem = pl.BlockSpec(memory_space=pltpu.SMEM)
    tile = pl.BlockSpec((ROUTE_ROWS, LANES), lambda c: (jnp.minimum(c, n_chunks - 1), 0))
    inv = pl.pallas_call(
        functools.partial(_route_kernel, n_chunks=n_chunks, n_asg=n_asg),
        grid=(n_chunks + 1,),
        in_specs=[smem, smem, smem, tile, tile],
        out_specs=smem,
        out_shape=jax.ShapeDtypeStruct((n_pad + LOOKAHEAD_BLOCKS * BM,), jnp.int32),
        scratch_shapes=[
            pltpu.VMEM((2, ROUTE_ROWS, LANES), jnp.int32),
            pltpu.SMEM((2, ROUTE_ROWS, LANES), jnp.int32),
            pltpu.SemaphoreType.DMA((2,)),
        ],
        compiler_params=pltpu.CompilerParams(dimension_semantics=("arbitrary",)),
        name="route_tables",
    )(pstart, pstart + counts, pend, eidx.reshape(n_asg // LANES, LANES),
      rank.reshape(n_asg // LANES, LANES))
    return inv, block_meta, n_blocks


def _expert_kernel(meta_ref, nb_ref, inv_hbm, h2_hbm, wgu_hbm, bgu_ref, wd_hbm, bd_ref, y4_hbm,
                   idx_smem, xbuf, ybuf, wgu_f32, wd_f32, wgu_bf, wd_bf, isem, gsem, ssem, zsem, wsem,
                   *, nb_max, n_tok, layer):
    i = pl.program_id(0)
    nb = nb_ref[0]
    s_cur = i % 3
    s_next = (i + 1) % 3
    s_prev = (i + 2) % 3
    q_ahead2 = (i + 2) % 5
    q_ahead3 = (i + 3) % 5
    q_prev = (i + 4) % 5
    rows = BM * SUBLANES
    d_ff = wd_bf.shape[0]
    n_asg = TOP_K * n_tok

    def idx_copy(blk, s):
        return pltpu.make_async_copy(inv_hbm.at[pl.ds(blk * BM, BM)], idx_smem.at[s], isem.at[s])

    def gather_copy(s_idx, g, r):
        tok = idx_smem[s_idx, r] & (n_tok - 1)
        return pltpu.make_async_copy(
            h2_hbm.at[pl.ds(pl.multiple_of(tok * SUBLANES, SUBLANES), SUBLANES)],
            xbuf.at[pl.ds(pl.multiple_of(g * rows + r * SUBLANES, SUBLANES), SUBLANES)],
            gsem.at[g])

    def scatter_copy(s_idx, s, r):
        dst = idx_smem[s_idx, r]
        return pltpu.make_async_copy(
            ybuf.at[pl.ds(pl.multiple_of(s * rows + r * SUBLANES, SUBLANES), SUBLANES)],
            y4_hbm.at[pl.ds(pl.multiple_of(dst * SUBLANES, SUBLANES), SUBLANES)],
            ssem.at[s])

    def wait_gather(g):
        pltpu.make_async_copy(h2_hbm.at[pl.ds(0, rows)], xbuf.at[pl.ds(g * rows, rows)],
                              gsem.at[g]).wait()

    def wait_scatter(s):
        pltpu.make_async_copy(ybuf.at[pl.ds(s * rows, rows)], y4_hbm.at[pl.ds(0, rows)],
                              ssem.at[s]).wait()

    @pl.when(i == 0)
    def _():
        ybuf[...] = jnp.zeros(ybuf.shape, F32)
        spare = pltpu.make_async_copy(
            ybuf.at[pl.ds(0, DUMP_BLOCKS * rows)],
            y4_hbm.at[pl.ds(n_asg * SUBLANES, DUMP_BLOCKS * rows)], zsem)
        spare.start()
        spare.wait()

        def fill(r, carry):
            idx_smem[4, r] = n_asg + 2 * BM + r
            return carry
        lax.fori_loop(0, BM, fill, 0)
        for b in range(2):
            idx_copy(b, b).start()
            idx_copy(b, b).wait()

            def body(r, carry, b=b):
                gather_copy(b, b, r).start()
                return carry
            lax.fori_loop(0, BM, body, 0)
        idx_copy(2, 2).start()

    def weight_copy(e, p):
        return _CopyGroup([
            pltpu.make_async_copy(wgu_hbm.at[layer, e], wgu_f32.at[p], wsem.at[p]),
            pltpu.make_async_copy(wd_hbm.at[layer, e], wd_f32.at[p], wsem.at[p])])

    @pl.when(i == 0)
    def _():
        weight_copy(meta_ref[0, 0], 0).start()

    @pl.when((i < nb) & (meta_ref[1, i] == 1))
    def _():
        p = meta_ref[2, i]
        weight_copy(meta_ref[0, i], p).wait()
        wgu_bf[...] = wgu_f32[p].astype(BF16)
        wd_bf[...] = wd_f32[p].astype(BF16)

        @pl.when(meta_ref[3, i] >= 0)
        def _():
            weight_copy(meta_ref[3, i], 1 - p).start()

    @pl.when((i < nb) & (i >= 2))
    def _():
        wait_scatter(s_cur)

    @pl.when(i < nb)
    def _():
        idx_copy(i + 3, q_ahead3).start()
        wait_gather(s_cur)
        idx_copy(i + 2, q_ahead2).wait()
        n_chunk = d_ff // MXU_DEPTH
        halves = MXU_DEPTH // LANES
        per_gather = -(-BM // (n_chunk - 1))
        per_scatter = BM // (2 * n_chunk)
        act_base = DUMP_BLOCKS * rows
        bgu = bgu_ref[0, 0]

        def scatter_group(g):
            for r in range(g * per_scatter, (g + 1) * per_scatter):
                scatter_copy(q_prev, s_prev, r).start(priority=r % 2)

        for c in range(n_chunk):
            xb = _load_row_tiled(xbuf, s_cur * rows, BM).astype(BF16)
            gcols = slice(c * MXU_DEPTH, (c + 1) * MXU_DEPTH)
            ucols = slice(d_ff + c * MXU_DEPTH, d_ff + (c + 1) * MXU_DEPTH)
            gt = jnp.dot(xb, wgu_bf[:, gcols], preferred_element_type=F32) + bgu[:, gcols]
            up = jnp.dot(xb, wgu_bf[:, ucols], preferred_element_type=F32) + bgu[:, ucols]
            gt = jnp.minimum(gt, SWIGLU_LIMIT)
            up = jnp.clip(up, -SWIGLU_LIMIT, SWIGLU_LIMIT)
            act_c = (up + 1.0) * gt * _sigmoid(gt * SWIGLU_ALPHA)
            if c < n_chunk - 1:
                for r in range(c * per_gather, min((c + 1) * per_gather, BM)):
                    gather_copy(q_ahead2, s_prev, r).start(priority=r % 2)
            scatter_group(c)
            for h in range(halves):
                ybuf[pl.ds(act_base + (c * halves + h) * BM, BM), :] = act_c[:, h * LANES:(h + 1) * LANES]
        act = jnp.concatenate(
            [ybuf[pl.ds(act_base + j * BM, BM), :] for j in range(n_chunk * halves)],
            axis=-1).astype(BF16)
        for c in range(n_chunk):
            cols = slice(c * MXU_DEPTH, (c + 1) * MXU_DEPTH)
            yc = jnp.dot(act, wd_bf[:, cols], preferred_element_type=F32) + bd_ref[0, 0][:, cols]
            scatter_group(n_chunk + c)
            for h in range(halves):
                ybuf[pl.ds(s_cur * rows + c * halves + h, BM, stride=SUBLANES), :] = (
                    yc[:, h * LANES:(h + 1) * LANES])

    @pl.when(i == nb)
    def _():
        wait_gather(s_cur)
        wait_gather(s_next)
        idx_copy(i + 2, q_ahead2).wait()

        @pl.when(i >= 2)
        def _():
            wait_scatter(s_cur)

        def body(r, carry):
            scatter_copy(q_prev, s_prev, r).start()
            return carry
        lax.fori_loop(0, BM, body, 0)
        wait_scatter(s_next)
        wait_scatter(s_prev)


def _expert_call(layer, block_meta, n_blocks, inv, h2, w_gu, b_gu, w_down, b_down):
    n_tok = h2.shape[0] // SUBLANES
    depth, n_exp, d, two_ff = w_gu.shape
    d_ff = two_ff // 2
    nb_max = inv.shape[0] // BM - LOOKAHEAD_BLOCKS
    rows = BM * SUBLANES
    grid_spec = pltpu.PrefetchScalarGridSpec(
        num_scalar_prefetch=2,
        grid=(nb_max,),
        in_specs=[
            pl.BlockSpec(memory_space=pl.ANY),
            pl.BlockSpec(memory_space=pl.ANY),
            pl.BlockSpec(memory_space=pl.ANY),
            pl.BlockSpec((1, 1, 1, two_ff), lambda i, meta, nb: (layer, meta[0, i], 0, 0)),
            pl.BlockSpec(memory_space=pl.ANY),
            pl.BlockSpec((1, 1, 1, d), lambda i, meta, nb: (layer, meta[0, i], 0, 0)),
        ],
        out_specs=pl.BlockSpec(memory_space=pl.ANY),
        scratch_shapes=[
            pltpu.SMEM((5, BM), jnp.int32),
            pltpu.VMEM((3 * rows, LANES), F32),
            pltpu.VMEM(((DUMP_BLOCKS + 1) * rows, LANES), F32),
            pltpu.VMEM((2, d, two_ff), F32),
            pltpu.VMEM((2, d_ff, d), F32),
            pltpu.VMEM((d, two_ff), BF16),
            pltpu.VMEM((d_ff, d), BF16),
            pltpu.SemaphoreType.DMA((5,)),
            pltpu.SemaphoreType.DMA((3,)),
            pltpu.SemaphoreType.DMA((3,)),
            pltpu.SemaphoreType.DMA(()),
            pltpu.SemaphoreType.DMA((2,)),
        ],
    )
    return pl.pallas_call(
        functools.partial(_expert_kernel, nb_max=nb_max, n_tok=n_tok, layer=layer),
        grid_spec=grid_spec,
        out_shape=jax.ShapeDtypeStruct(((TOP_K * n_tok + DUMP_BLOCKS * BM) * SUBLANES, LANES), F32),
        compiler_params=pltpu.CompilerParams(dimension_semantics=("arbitrary",),
                                             vmem_limit_bytes=VMEM_LIMIT),
        name="experts",
    )(block_meta, n_blocks, inv, h2, w_gu, b_gu.reshape(depth, n_exp, 1, two_ff), w_down,
      b_down.reshape(depth, n_exp, 1, d))


def kernel(x, c, ada_w, ada_b, mix_pre_g, mix_post_g, w_in, sc_conv, sc_out, cf_conv, cf_conv_b, cf_ln_g, cf_ln_b, cf_out, cf_out_b, pool_w, pool_scale, w_o, moe_pre_g, moe_post_g, w_router, b_router, w_gu, b_gu, w_down, b_down):
    bsz, seq_len, d = x.shape
    depth = ada_w.shape[0]
    n_tok = bsz * seq_len
    assert d == SUBLANES * LANES and seq_len % TM == 0 and d % COL_CHUNK == 0 and TM % ROW_CHUNK == 0
    assert (n_tok * TOP_K) % (ROUTE_ROWS * LANES) == 0 and (n_tok * TOP_K) % BM == 0
    assert n_tok >= 2 * BM and n_tok & (n_tok - 1) == 0 and BM & (BM - 1) == 0
    mods = _ada_call(c, ada_w, ada_b)
    xs = x.reshape(n_tok, d)
    comb = None
    for l in range(depth):
        p = dict(
            pre_g=mix_pre_g[l], post_g=mix_post_g[l], w_in=w_in[l].astype(BF16), sc_conv=sc_conv[l],
            sc_out=sc_out[l].astype(BF16), cf_conv=cf_conv[l], cf_conv_b=cf_conv_b[l], cf_ln_g=cf_ln_g[l],
            cf_ln_b=cf_ln_b[l], cf_out=cf_out[l].astype(BF16), cf_out_b=cf_out_b[l],
            pool_w=pool_w[l].astype(BF16), pool_scale=pool_scale[l], w_o=w_o[l].astype(BF16),
            moe_pre_g=moe_pre_g[l], w_router_t=_split_bf16(w_router[l].T), b_router=b_router[l],
        )
        x1, h2, eidx, gate, rank, counts = _mix_call(xs if comb is None else None, comb, mods[l], p, seq_len)
        inv, block_e, n_blocks = _routing_tables(eidx, rank, counts[:, 0])
        y4 = _expert_call(l, block_e, n_blocks, inv, h2, w_gu, b_gu, w_down, b_down)
        comb = (x1, y4, gate.T, mods[l], moe_post_g[l])
    return _final_call(comb, seq_len).reshape(bsz, seq_len, d)
```

```python
import functools

import jax
import jax.numpy as jnp
from jax import lax
from jax.experimental import pallas as pl
from jax.experimental.pallas import tpu as pltpu

F32 = jnp.float32
BF16 = jnp.bfloat16

N_EXPERTS = 32
TOP_K = 4
N_MOD = 6
SC_WIDTH = 3
CF_WIDTH = 31
POOL_WINDOWS = (2, 4, 8, 16)
SWIGLU_LIMIT = 7.0
SWIGLU_ALPHA = 1.702
NORM_EPS = 1e-6
LN_EPS = 1e-5

SUBLANES = 8
LANES = 128
MXU_DEPTH = 256
TM = 256
BM = 256
SC_HALO = 8
CF_HALO = 32
POOL_HALO = 16
ROW_CHUNK = 32
COL_CHUNK = 256
ROUTE_ROWS = 64
DUMP_BLOCKS = 3
LOOKAHEAD_BLOCKS = 2
VMEM_LIMIT = 60 * 1024 * 1024


def _rms(x, g):
    return x * lax.rsqrt(jnp.mean(x * x, axis=-1, keepdims=True) + NORM_EPS) * g


def _sigmoid(x):
    return 0.5 * jnp.tanh(0.5 * x) + 0.5


def _const_spec(shape):
    nd = len(shape)
    return pl.BlockSpec(shape, lambda *_: (0,) * nd, pipeline_mode=pl.Buffered(1))


class _CopyGroup:
    def __init__(self, copies):
        self.copies = copies

    def start(self):
        for c in self.copies:
            c.start()

    def wait(self):
        for c in self.copies:
            c.wait()


def _load_row_tiled(ref, base, rows):
    return jnp.concatenate(
        [ref[pl.ds(base + s, rows, stride=SUBLANES), :] for s in range(SUBLANES)], axis=-1)


def _store_row_tiled(ref, base, value):
    rows = value.shape[0]
    for s in range(SUBLANES):
        ref[pl.ds(base + s, rows, stride=SUBLANES), :] = value[:, s * LANES:(s + 1) * LANES]


def _ada_kernel(c_ref, w_ref, b_ref, o_ref):
    c = c_ref[...]
    c_act = c * _sigmoid(c)
    o_ref[0, 0] = jnp.dot(c_act, w_ref[0], preferred_element_type=F32,
                          precision=lax.Precision.HIGHEST) + b_ref[0, 0]


def _ada_call(c, ada_w, ada_b):
    depth, d, _ = ada_w.shape
    bsz = c.shape[0]
    ada_b4 = ada_b.reshape(depth, N_MOD, 1, d)
    out = pl.pallas_call(
        _ada_kernel,
        grid=(depth, N_MOD),
        in_specs=[
            pl.BlockSpec((bsz, d), lambda l, j: (0, 0)),
            pl.BlockSpec((1, d, d), lambda l, j: (l, 0, j)),
            pl.BlockSpec((1, 1, 1, d), lambda l, j: (l, j, 0, 0)),
        ],
        out_specs=pl.BlockSpec((1, 1, bsz, d), lambda l, j: (l, j, 0, 0)),
        out_shape=jax.ShapeDtypeStruct((depth, N_MOD, bsz, d), F32),
        name="ada_mod",
    )(c, ada_w, ada_b4)
    return jnp.transpose(out, (0, 2, 1, 3))


def _combine(x1_ref, y4_refs, gate_ref, mod_ref, post_g):
    tm = x1_ref.shape[0]
    gate = gate_ref[...]
    y = None
    for k, y_ref in enumerate(y4_refs):
        term = gate[:, k:k + 1] * _load_row_tiled(y_ref, 0, tm)
        y = term if y is None else y + term
    return x1_ref[...] + mod_ref[0, 5:6, :] * _rms(y, post_g[...])


def _combine_specs(n, d, steps_per_seq):
    blocks_per_plane = n // TM
    specs = [pl.BlockSpec((TM, d), lambda i: (i, 0))]
    for k in range(TOP_K):
        specs.append(pl.BlockSpec((TM * SUBLANES, LANES),
                                  lambda i, k=k: (k * blocks_per_plane + i, 0)))
    specs += [
        pl.BlockSpec((TM, TOP_K), lambda i: (i, 0)),
        pl.BlockSpec((1, N_MOD, d), lambda i: (i // steps_per_seq, 0, 0)),
        pl.BlockSpec((1, d), lambda i: (0, 0)),
    ]
    return specs


def _final_kernel(x1_ref, y0, y1, y2, y3, gate_ref, mod_ref, post_g, out_ref):
    out_ref[...] = _combine(x1_ref, (y0, y1, y2, y3), gate_ref, mod_ref, post_g)


def _final_call(comb, seq_len):
    x1, y4, gate_nk, mod, post_g = comb
    n, d = x1.shape
    return pl.pallas_call(
        _final_kernel,
        grid=(n // TM,),
        in_specs=_combine_specs(n, d, seq_len // TM),
        out_specs=pl.BlockSpec((TM, d), lambda i: (i, 0)),
        out_shape=jax.ShapeDtypeStruct((n, d), F32),
        compiler_params=pltpu.CompilerParams(dimension_semantics=("arbitrary",),
                                             vmem_limit_bytes=VMEM_LIMIT),
        name="final_combine",
    )(x1, y4, y4, y4, y4, gate_nk, mod, post_g.reshape(1, d))


def _causal_conv(buf_ref, w_ref, width, halo, out_ref, shift_ref, bias_ref=None, before_chunk=()):
    tm, d = out_ref.shape
    n_rows = buf_ref.shape[0] - SUBLANES
    offsets = [halo - (width - 1) + k for k in range(width)]
    for ci, c0 in enumerate(range(0, d, COL_CHUNK)):
        if ci < len(before_chunk):
            before_chunk[ci]()
        cols = slice(c0, c0 + COL_CHUNK)
        for q in sorted({o % SUBLANES for o in offsets} - {0}):
            shift_ref[q - 1, 0:n_rows, :] = buf_ref[q:q + n_rows, cols]
        for r0 in range(0, tm, ROW_CHUNK):
            accs = [None] * (ROW_CHUNK // SUBLANES)
            for k, o in enumerate(offsets):
                q = o % SUBLANES
                w8 = w_ref[k, :, cols]
                for j in range(len(accs)):
                    start = o - q + r0 + j * SUBLANES
                    if q == 0:
                        rows = buf_ref[start:start + SUBLANES, cols]
                    else:
                        rows = shift_ref[q - 1, start:start + SUBLANES, :]
                    accs[j] = w8 * rows if accs[j] is None else accs[j] + w8 * rows
            for j, acc in enumerate(accs):
                if bias_ref is not None:
                    acc = acc + bias_ref[:, cols]
                out_ref[r0 + j * SUBLANES:r0 + (j + 1) * SUBLANES, cols] = acc


def _mix_kernel(*refs, steps_per_seq, has_comb):
    n_in = 8 if has_comb else 1
    x_in = refs[:n_in]
    (mod_ref, pre_g, post_g, w_in, sc_conv, sc_out, cf_conv, cf_conv_b, cf_ln_g, cf_ln_b, cf_out,
     cf_out_b, pool_w, pool_scale, w_o, moe_pre_g, w_router_t, b_router, band_cur, band_prev,
     x1_ref, h2_ref, eidx_ref, gate_ref, rank_ref, counts_ref,
     cx_buf, v_buf, u_buf, tmp_buf, tmp2_buf, shift_buf, carry_ref) = refs[n_in:]
    i = pl.program_id(0)
    t = i % steps_per_seq
    tm, d = x1_ref.shape

    @pl.when(t == 0)
    def _():
        cx_buf[0:SC_HALO, :] = jnp.zeros((SC_HALO, d), F32)
        v_buf[0:CF_HALO, :] = jnp.zeros((CF_HALO, d), F32)
        u_buf[...] = jnp.zeros((POOL_HALO, d), F32)

    @pl.when(i == 0)
    def _():
        carry_ref[...] = jnp.zeros(carry_ref.shape, F32)

    if has_comb:
        xp_ref, y0, y1, y2, y3, gate_prev, mod_prev, post_g_prev = x_in
        x = _combine(xp_ref, (y0, y1, y2, y3), gate_prev, mod_prev, post_g_prev)
    else:
        x = x_in[0][...]
    shift1, scale1, gate1 = mod_ref[0, 0:1, :], mod_ref[0, 1:2, :], mod_ref[0, 2:3, :]
    shift2, scale2 = mod_ref[0, 3:4, :], mod_ref[0, 4:5, :]

    hb = (_rms(x, pre_g[...]) * (1.0 + scale1) + shift1).astype(BF16)

    def proj(j):
        return jnp.dot(hb, w_in[:, j * d:(j + 1) * d], preferred_element_type=F32)

    v_buf[CF_HALO:CF_HALO + tm, :] = proj(3) * _sigmoid(proj(4))
    side = {}

    def pool_in():
        side["u"] = proj(5)

    def branch_a():
        cx_buf[SC_HALO:SC_HALO + tm, :] = proj(1) * proj(2)
        _causal_conv(cx_buf, sc_conv, SC_WIDTH, SC_HALO, tmp2_buf, shift_buf)
        cx_buf[0:SC_HALO, :] = cx_buf[tm:tm + SC_HALO, :]
        conv_a = (proj(0) * tmp2_buf[...]).astype(BF16)
        side["merged"] = _sigmoid(proj(6)) * jnp.dot(conv_a, sc_out[...], preferred_element_type=F32)

    def gate_b():
        side["gate_b"] = _sigmoid(proj(7))

    def gate_c():
        side["gate_c"] = _sigmoid(proj(8))

    _causal_conv(v_buf, cf_conv, CF_WIDTH, CF_HALO, tmp_buf, shift_buf, cf_conv_b,
                 before_chunk=(pool_in, branch_a, gate_b, gate_c))
    v_buf[0:CF_HALO, :] = v_buf[tm:tm + CF_HALO, :]
    v = tmp_buf[...]
    mu = jnp.mean(v, axis=-1, keepdims=True)
    vc = v - mu
    var = jnp.mean(vc * vc, axis=-1, keepdims=True)
    vn = vc * lax.rsqrt(var + LN_EPS) * cf_ln_g[...] + cf_ln_b[...]
    vn = vn * _sigmoid(vn)
    y_b = jnp.dot(vn.astype(BF16), cf_out[...], preferred_element_type=F32) + cf_out_b[...]
    merged = side["merged"] + side["gate_b"] * y_b

    u = side["u"]
    pos = (t * tm + 1 + lax.broadcasted_iota(jnp.int32, (tm, 1), 0)).astype(F32)
    group = d // len(POOL_WINDOWS)

    def split(a):
        hi = a.astype(BF16)
        return jnp.concatenate([hi, (a - hi.astype(F32)).astype(BF16)], axis=-1)

    y_c_parts = []
    for g, w in enumerate(POOL_WINDOWS):
        cols = slice(g * group, (g + 1) * group)
        s2 = jnp.dot(band_cur[g], split(u[:, cols]), preferred_element_type=F32)
        h2s = jnp.dot(band_prev[g], split(u_buf[:, cols]), preferred_element_type=F32)
        s = s2[:, :group] + s2[:, group:]
        s = jnp.concatenate([s[:POOL_HALO] + h2s[:, :group] + h2s[:, group:], s[POOL_HALO:]], axis=0)
        p = s * (1.0 / jnp.minimum(pos, float(w))) - u[:, cols]
        y_c_parts.append(jnp.dot(p.astype(BF16), pool_w[g], preferred_element_type=F32))
    u_buf[...] = u[tm - POOL_HALO:, :]
    y_c = jnp.concatenate(y_c_parts, axis=-1) * pool_scale[...]
    merged = merged + side["gate_c"] * y_c

    y = jnp.dot(merged.astype(BF16), w_o[...], preferred_element_type=F32)
    x1 = x + gate1 * _rms(y, post_g[...])
    x1_ref[...] = x1

    h2 = _rms(x1, moe_pre_g[...]) * (1.0 + scale2) + shift2
    _store_row_tiled(h2_ref, 0, h2)
    h2_hi = h2.astype(BF16)
    h2_split = jnp.concatenate([h2_hi, (h2 - h2_hi.astype(F32)).astype(BF16)], axis=0)
    cross = lax.dot_general(w_router_t[...], h2_split, (((1,), (1,)), ((), ())),
                            preferred_element_type=F32)
    logits = (cross[:N_EXPERTS, :tm] + cross[N_EXPERTS:, :tm] + cross[:N_EXPERTS, tm:]
              + b_router[...])
    eio = lax.broadcasted_iota(jnp.int32, (N_EXPERTS, tm), 0)
    neg_inf = jnp.float32(-jnp.inf)
    vals, idxs = [], []
    for _ in range(TOP_K):
        m = jnp.max(logits, axis=0, keepdims=True)
        idx = jnp.min(jnp.where(logits == m, eio, N_EXPERTS), axis=0, keepdims=True)
        vals.append(m)
        idxs.append(idx)
        logits = jnp.where(eio == idx, neg_inf, logits)
    exps = [jnp.exp(v - vals[0]) for v in vals]
    denom = exps[0] + exps[1] + exps[2] + exps[3]
    onehot = jnp.zeros((N_EXPERTS, tm), F32)
    for idx in idxs:
        onehot = onehot + (eio == idx).astype(F32)
    tri = (lax.broadcasted_iota(jnp.int32, (tm, tm), 0)
           < lax.broadcasted_iota(jnp.int32, (tm, tm), 1)).astype(BF16)
    carry = carry_ref[:, 0:1]
    before = jnp.dot(onehot.astype(BF16), tri, preferred_element_type=F32) + carry
    for k in range(TOP_K):
        eidx_ref[k:k + 1, :] = idxs[k]
        gate_ref[k:k + 1, :] = exps[k] / denom
        rank_ref[k:k + 1, :] = jnp.sum(jnp.where(eio == idxs[k], before, 0.0), axis=0,
                                       keepdims=True).astype(jnp.int32)
    carry = carry + jnp.sum(onehot, axis=1, keepdims=True)
    carry_ref[...] = jnp.broadcast_to(carry, carry_ref.shape)
    counts_ref[...] = jnp.broadcast_to(carry, counts_ref.shape).astype(jnp.int32)


def _split_bf16(a):
    hi = a.astype(BF16)
    return jnp.concatenate([hi, (a - hi.astype(F32)).astype(BF16)], axis=0)


def _pool_bands():
    t = jnp.arange(TM)[:, None]
    j = jnp.arange(TM)[None, :]
    th = jnp.arange(POOL_HALO)[:, None]
    jh = jnp.arange(POOL_HALO)[None, :] - POOL_HALO
    cur = jnp.stack([(j <= t) & (j > t - w) for w in POOL_WINDOWS]).astype(BF16)
    prev = jnp.stack([jh > th - w for w in POOL_WINDOWS]).astype(BF16)
    return cur, prev


def _mix_call(x2d, comb, mod, p, seq_len):
    has_comb = comb is not None
    n, d = comb[0].shape if has_comb else x2d.shape
    steps_per_seq = seq_len // TM
    n_steps = n // TM
    row = lambda a: a.reshape(1, d)
    taps = lambda a: jnp.broadcast_to(a[:, None, :], (a.shape[0], SUBLANES, d))
    if has_comb:
        x1p, y4p, gate_p, mod_p, post_g_p = comb
        lead_args = (x1p, y4p, y4p, y4p, y4p, gate_p, mod_p, row(post_g_p))
        lead_specs = _combine_specs(n, d, steps_per_seq)
    else:
        lead_args = (x2d,)
        lead_specs = [pl.BlockSpec((TM, d), lambda i: (i, 0))]
    const_args = (
        row(p["pre_g"]), row(p["post_g"]), p["w_in"], taps(p["sc_conv"]), p["sc_out"], taps(p["cf_conv"]),
        row(p["cf_conv_b"]), row(p["cf_ln_g"]), row(p["cf_ln_b"]), p["cf_out"], row(p["cf_out_b"]),
        p["pool_w"], row(p["pool_scale"]), p["w_o"], row(p["moe_pre_g"]), p["w_router_t"],
        p["b_router"].reshape(N_EXPERTS, 1), *_pool_bands(),
    )
    in_specs = (lead_specs + [pl.BlockSpec((1, N_MOD, d), lambda i: (i // steps_per_seq, 0, 0))]
                + [_const_spec(a.shape) for a in const_args])
    tok_spec = pl.BlockSpec((TOP_K, TM), lambda i: (0, i))
    out_shape = (
        jax.ShapeDtypeStruct((n, d), F32),
        jax.ShapeDtypeStruct((n * SUBLANES, LANES), F32),
        jax.ShapeDtypeStruct((TOP_K, n), jnp.int32),
        jax.ShapeDtypeStruct((TOP_K, n), F32),
        jax.ShapeDtypeStruct((TOP_K, n), jnp.int32),
        jax.ShapeDtypeStruct((N_EXPERTS, LANES), jnp.int32),
    )
    out_specs = (
        pl.BlockSpec((TM, d), lambda i: (i, 0)),
        pl.BlockSpec((TM * SUBLANES, LANES), lambda i: (i, 0)),
        tok_spec, tok_spec, tok_spec,
        pl.BlockSpec((N_EXPERTS, LANES), lambda i: (0, 0)),
    )
    return pl.pallas_call(
        functools.partial(_mix_kernel, steps_per_seq=steps_per_seq, has_comb=has_comb),
        grid=(n_steps,),
        in_specs=in_specs,
        out_specs=out_specs,
        out_shape=out_shape,
        scratch_shapes=[
            pltpu.VMEM((SC_HALO + TM, d), F32),
            pltpu.VMEM((CF_HALO + TM, d), F32),
            pltpu.VMEM((POOL_HALO, d), F32),
            pltpu.VMEM((TM, d), F32),
            pltpu.VMEM((TM, d), F32),
            pltpu.VMEM((SUBLANES - 1, CF_HALO + TM - SUBLANES, COL_CHUNK), F32),
            pltpu.VMEM((N_EXPERTS, LANES), F32),
        ],
        compiler_params=pltpu.CompilerParams(dimension_semantics=("arbitrary",),
                                             vmem_limit_bytes=VMEM_LIMIT),
        name="mixer_router",
    )(*lead_args, mod, *const_args)


def _route_kernel(pstart_ref, pvalid_ref, pend_ref, eidx_ref, rank_ref, inv_ref,
                  dest_vmem, dest_smem, sem, *, n_chunks, n_asg):
    c = pl.program_id(0)
    n_pad = inv_ref.shape[0]
    chunk = ROUTE_ROWS * LANES

    def to_smem(s):
        return pltpu.make_async_copy(dest_vmem.at[s], dest_smem.at[s], sem.at[s])

    @pl.when(c == 0)
    def _():
        def pad_slot(p, carry):
            inv_ref[p] = n_asg + (p & (2 * BM - 1))
            return carry

        def pad_expert(e, carry):
            lax.fori_loop(pvalid_ref[e], pend_ref[e], pad_slot, 0)
            return carry

        lax.fori_loop(0, N_EXPERTS, pad_expert, 0)
        lax.fori_loop(pend_ref[N_EXPERTS - 1], n_pad, pad_slot, 0)

    @pl.when(c < n_chunks)
    def _():
        eidx = eidx_ref[...]
        dest = rank_ref[...]
        for e in range(N_EXPERTS):
            dest = dest + jnp.where(eidx == e, pstart_ref[e], 0)
        dest_vmem[c % 2] = dest
        to_smem(c % 2).start()

    @pl.when(c >= 1)
    def _():
        s = (c - 1) % 2
        to_smem(s).wait()
        base = (c - 1) * chunk

        def row(r, carry):
            for col in range(LANES):
                inv_ref[dest_smem[s, r, col]] = base + r * LANES + col
            return carry

        lax.fori_loop(0, ROUTE_ROWS, row, 0)


def _routing_tables(eidx, rank, counts):
    n = eidx.shape[1]
    n_asg = n * TOP_K
    n_pad = n_asg + N_EXPERTS * BM
    nb_max = n_pad // BM
    chunk = ROUTE_ROWS * LANES
    n_chunks = n_asg // chunk
    padded = (counts + BM - 1) // BM * BM
    pend = jnp.cumsum(padded).astype(jnp.int32)
    pstart = pend - padded
    n_blocks = pend[-1:] // BM
    blk = jnp.arange(nb_max, dtype=jnp.int32)
    block_e = jnp.minimum(jnp.sum((pend[None, :] <= blk[:, None] * BM).astype(jnp.int32), axis=1),
                          N_EXPERTS - 1)
    last_e = jnp.sum(jnp.where(blk == n_blocks[0] - 1, block_e, 0))
    block_e = jnp.where(blk < n_blocks[0], block_e, last_e).astype(jnp.int32)
    used = blk < n_blocks[0]
    first = jnp.concatenate([jnp.ones((1,), bool), block_e[1:] != block_e[:-1]]) & used
    parity = (jnp.cumsum(first.astype(jnp.int32)) - 1) % 2
    later = (block_e[None, :] > block_e[:, None]) & used[None, :]
    next_e = jnp.min(jnp.where(later, block_e[None, :], N_EXPERTS), axis=1)
    next_e = jnp.where(next_e == N_EXPERTS, -1, next_e)
    block_meta = jnp.stack([block_e, first.astype(jnp.int32), parity, next_e]).astype(jnp.int32)
    smem = pl.BlockSpec(memory_space=pltpu.SMEM)
    tile = pl.BlockSpec((ROUTE_ROWS, LANES), lambda c: (jnp.minimum(c, n_chunks - 1), 0))
    inv = pl.pallas_call(
        functools.partial(_route_kernel, n_chunks=n_chunks, n_asg=n_asg),
        grid=(n_chunks + 1,),
        in_specs=[smem, smem, smem, tile, tile],
        out_specs=smem,
        out_shape=jax.ShapeDtypeStruct((n_pad + LOOKAHEAD_BLOCKS * BM,), jnp.int32),
        scratch_shapes=[
            pltpu.VMEM((2, ROUTE_ROWS, LANES), jnp.int32),
            pltpu.SMEM((2, ROUTE_ROWS, LANES), jnp.int32),
            pltpu.SemaphoreType.DMA((2,)),
        ],
        compiler_params=pltpu.CompilerParams(dimension_semantics=("arbitrary",)),
        name="route_tables",
    )(pstart, pstart + counts, pend, eidx.reshape(n_asg // LANES, LANES),
      rank.reshape(n_asg // LANES, LANES))
    return inv, block_meta, n_blocks


def _expert_kernel(meta_ref, nb_ref, inv_hbm, h2_hbm, wgu_hbm, bgu_ref, wd_hbm, bd_ref, y4_hbm,
                   idx_smem, xbuf, ybuf, wgu_f32, wd_f32, wgu_bf, wd_bf, isem, gsem, ssem, zsem, wsem,
                   *, nb_max, n_tok, layer):
    i = pl.program_id(0)
    nb = nb_ref[0]
    s_cur = i % 3
    s_next = (i + 1) % 3
    s_prev = (i + 2) % 3
    q_ahead2 = (i + 2) % 5
    q_ahead3 = (i + 3) % 5
    q_prev = (i + 4) % 5
    rows = BM * SUBLANES
    d_ff = wd_bf.shape[0]
    n_asg = TOP_K * n_tok

    def idx_copy(blk, s):
        return pltpu.make_async_copy(inv_hbm.at[pl.ds(blk * BM, BM)], idx_smem.at[s], isem.at[s])

    def gather_copy(s_idx, g, r):
        tok = idx_smem[s_idx, r] & (n_tok - 1)
        return pltpu.make_async_copy(
            h2_hbm.at[pl.ds(pl.multiple_of(tok * SUBLANES, SUBLANES), SUBLANES)],
            xbuf.at[pl.ds(pl.multiple_of(g * rows + r * SUBLANES, SUBLANES), SUBLANES)],
            gsem.at[g])

    def scatter_copy(s_idx, s, r):
        dst = idx_smem[s_idx, r]
        return pltpu.make_async_copy(
            ybuf.at[pl.ds(pl.multiple_of(s * rows + r * SUBLANES, SUBLANES), SUBLANES)],
            y4_hbm.at[pl.ds(pl.multiple_of(dst * SUBLANES, SUBLANES), SUBLANES)],
            ssem.at[s])

    def wait_gather(g):
        pltpu.make_async_copy(h2_hbm.at[pl.ds(0, rows)], xbuf.at[pl.ds(g * rows, rows)],
                              gsem.at[g]).wait()

    def wait_scatter(s):
        pltpu.make_async_copy(ybuf.at[pl.ds(s * rows, rows)], y4_hbm.at[pl.ds(0, rows)],
                              ssem.at[s]).wait()

    @pl.when(i == 0)
    def _():
        ybuf[...] = jnp.zeros(ybuf.shape, F32)
        spare = pltpu.make_async_copy(
            ybuf.at[pl.ds(0, DUMP_BLOCKS * rows)],
            y4_hbm.at[pl.ds(n_asg * SUBLANES, DUMP_BLOCKS * rows)], zsem)
        spare.start()
        spare.wait()

        def fill(r, carry):
            idx_smem[4, r] = n_asg + 2 * BM + r
            return carry
        lax.fori_loop(0, BM, fill, 0)
        for b in range(2):
            idx_copy(b, b).start()
            idx_copy(b, b).wait()

            def body(r, carry, b=b):
                gather_copy(b, b, r).start()
                return carry
            lax.fori_loop(0, BM, body, 0)
        idx_copy(2, 2).start()

    def weight_copy(e, p):
        return _CopyGroup([
            pltpu.make_async_copy(wgu_hbm.at[layer, e], wgu_f32.at[p], wsem.at[p]),
            pltpu.make_async_copy(wd_hbm.at[layer, e], wd_f32.at[p], wsem.at[p])])

    @pl.when(i == 0)
    def _():
        weight_copy(meta_ref[0, 0], 0).start()

    @pl.when((i < nb) & (meta_ref[1, i] == 1))
    def _():
        p = meta_ref[2, i]
        weight_copy(meta_ref[0, i], p).wait()
        wgu_bf[...] = wgu_f32[p].astype(BF16)
        wd_bf[...] = wd_f32[p].astype(BF16)

        @pl.when(meta_ref[3, i] >= 0)
        def _():
            weight_copy(meta_ref[3, i], 1 - p).start()

    @pl.when((i < nb) & (i >= 2))
    def _():
        wait_scatter(s_cur)

    @pl.when(i < nb)
    def _():
        idx_copy(i + 3, q_ahead3).start()
        wait_gather(s_cur)
        idx_copy(i + 2, q_ahead2).wait()
        n_chunk = d_ff // MXU_DEPTH
        halves = MXU_DEPTH // LANES
        per_gather = BM // (2 * n_chunk)
        per_scatter = BM // (2 * n_chunk)
        act_base = DUMP_BLOCKS * rows
        bgu = bgu_ref[0, 0]

        def scatter_group(g):
            for r in range(g * per_scatter, (g + 1) * per_scatter):
                scatter_copy(q_prev, s_prev, r).start(priority=r % 2)

        def gather_group(g):
            for r in range(g * per_gather, (g + 1) * per_gather):
                gather_copy(q_ahead2, s_prev, r).start(priority=r % 2)

        def load_x_half(h):
            base = s_cur * rows + h * (SUBLANES // 2)
            return [xbuf[pl.ds(base + s, BM, stride=SUBLANES), :] for s in range(SUBLANES // 2)]

        for c in range(n_chunk):
            lo = load_x_half(0)
            gather_group(2 * c)
            hi = load_x_half(1)
            gather_group(2 * c + 1)
            xb = jnp.concatenate(lo + hi, axis=-1).astype(BF16)
            gcols = slice(c * MXU_DEPTH, (c + 1) * MXU_DEPTH)
            ucols = slice(d_ff + c * MXU_DEPTH, d_ff + (c + 1) * MXU_DEPTH)
            gt = jnp.dot(xb, wgu_bf[:, gcols], preferred_element_type=F32) + bgu[:, gcols]
            up = jnp.dot(xb, wgu_bf[:, ucols], preferred_element_type=F32) + bgu[:, ucols]
            gt = jnp.minimum(gt, SWIGLU_LIMIT)
            up = jnp.clip(up, -SWIGLU_LIMIT, SWIGLU_LIMIT)
            act_c = (up + 1.0) * gt * _sigmoid(gt * SWIGLU_ALPHA)
            for h in range(halves):
                if h == 0:
                    scatter_group(c)
                elif c == n_chunk - 1:
                    scatter_group(n_chunk)
                ybuf[pl.ds(act_base + (c * halves + h) * BM, BM), :] = act_c[:, h * LANES:(h + 1) * LANES]
        act = jnp.concatenate(
            [ybuf[pl.ds(act_base + j * BM, BM), :] for j in range(n_chunk * halves)],
            axis=-1).astype(BF16)
        for c in range(n_chunk):
            cols = slice(c * MXU_DEPTH, (c + 1) * MXU_DEPTH)
            yc = jnp.dot(act, wd_bf[:, cols], preferred_element_type=F32) + bd_ref[0, 0][:, cols]
            if c < n_chunk - 1:
                scatter_group(n_chunk + 1 + c)
            for h in range(halves):
                ybuf[pl.ds(s_cur * rows + c * halves + h, BM, stride=SUBLANES), :] = (
                    yc[:, h * LANES:(h + 1) * LANES])

    @pl.when(i == nb)
    def _():
        wait_gather(s_cur)
        wait_gather(s_next)
        idx_copy(i + 2, q_ahead2).wait()

        @pl.when(i >= 2)
        def _():
            wait_scatter(s_cur)

        def body(r, carry):
            scatter_copy(q_prev, s_prev, r).start()
            return carry
        lax.fori_loop(0, BM, body, 0)
        wait_scatter(s_next)
        wait_scatter(s_prev)


def _expert_call(layer, block_meta, n_blocks, inv, h2, w_gu, b_gu, w_down, b_down):
    n_tok = h2.shape[0] // SUBLANES
    depth, n_exp, d, two_ff = w_gu.shape
    d_ff = two_ff // 2
    nb_max = inv.shape[0] // BM - LOOKAHEAD_BLOCKS
    rows = BM * SUBLANES
    grid_spec = pltpu.PrefetchScalarGridSpec(
        num_scalar_prefetch=2,
        grid=(nb_max,),
        in_specs=[
            pl.BlockSpec(memory_space=pl.ANY),
            pl.BlockSpec(memory_space=pl.ANY),
            pl.BlockSpec(memory_space=pl.ANY),
            pl.BlockSpec((1, 1, 1, two_ff), lambda i, meta, nb: (layer, meta[0, i], 0, 0)),
            pl.BlockSpec(memory_space=pl.ANY),
            pl.BlockSpec((1, 1, 1, d), lambda i, meta, nb: (layer, meta[0, i], 0, 0)),
        ],
        out_specs=pl.BlockSpec(memory_space=pl.ANY),
        scratch_shapes=[
            pltpu.SMEM((5, BM), jnp.int32),
            pltpu.VMEM((3 * rows, LANES), F32),
            pltpu.VMEM(((DUMP_BLOCKS + 1) * rows, LANES), F32),
            pltpu.VMEM((2, d, two_ff), F32),
            pltpu.VMEM((2, d_ff, d), F32),
            pltpu.VMEM((d, two_ff), BF16),
            pltpu.VMEM((d_ff, d), BF16),
            pltpu.SemaphoreType.DMA((5,)),
            pltpu.SemaphoreType.DMA((3,)),
            pltpu.SemaphoreType.DMA((3,)),
            pltpu.SemaphoreType.DMA(()),
            pltpu.SemaphoreType.DMA((2,)),
        ],
    )
    return pl.pallas_call(
        functools.partial(_expert_kernel, nb_max=nb_max, n_tok=n_tok, layer=layer),
        grid_spec=grid_spec,
        out_shape=jax.ShapeDtypeStruct(((TOP_K * n_tok + DUMP_BLOCKS * BM) * SUBLANES, LANES), F32),
        compiler_params=pltpu.CompilerParams(dimension_semantics=("arbitrary",),
                                             vmem_limit_bytes=VMEM_LIMIT),
        name="experts",
    )(block_meta, n_blocks, inv, h2, w_gu, b_gu.reshape(depth, n_exp, 1, two_ff), w_down,
      b_down.reshape(depth, n_exp, 1, d))


def kernel(x, c, ada_w, ada_b, mix_pre_g, mix_post_g, w_in, sc_conv, sc_out, cf_conv, cf_conv_b, cf_ln_g, cf_ln_b, cf_out, cf_out_b, pool_w, pool_scale, w_o, moe_pre_g, moe_post_g, w_router, b_router, w_gu, b_gu, w_down, b_down):
    bsz, seq_len, d = x.shape
    depth = ada_w.shape[0]
    n_tok = bsz * seq_len
    assert d == SUBLANES * LANES and seq_len % TM == 0 and d % COL_CHUNK == 0 and TM % ROW_CHUNK == 0
    assert (n_tok * TOP_K) % (ROUTE_ROWS * LANES) == 0 and (n_tok * TOP_K) % BM == 0
    assert n_tok >= 2 * BM and n_tok & (n_tok - 1) == 0 and BM & (BM - 1) == 0
    mods = _ada_call(c, ada_w, ada_b)
    xs = x.reshape(n_tok, d)
    comb = None
    for l in range(depth):
        p = dict(
            pre_g=mix_pre_g[l], post_g=mix_post_g[l], w_in=w_in[l].astype(BF16), sc_conv=sc_conv[l],
            sc_out=sc_out[l].astype(BF16), cf_conv=cf_conv[l], cf_conv_b=cf_conv_b[l], cf_ln_g=cf_ln_g[l],
            cf_ln_b=cf_ln_b[l], cf_out=cf_out[l].astype(BF16), cf_out_b=cf_out_b[l],
            pool_w=pool_w[l].astype(BF16), pool_scale=pool_scale[l], w_o=w_o[l].astype(BF16),
            moe_pre_g=moe_pre_g[l], w_router_t=_split_bf16(w_router[l].T), b_router=b_router[l],
        )
        x1, h2, eidx, gate, rank, counts = _mix_call(xs if comb is None else None, comb, mods[l], p, seq_len)
        inv, block_e, n_blocks = _routing_tables(eidx, rank, counts[:, 0])
        y4 = _expert_call(l, block_e, n_blocks, inv, h2, w_gu, b_gu, w_down, b_down)
        comb = (x1, y4, gate.T, mods[l], moe_post_g[l])
    return _final_call(comb, seq_len).reshape(bsz, seq_len, d)
```

```python
import functools

import jax
import jax.numpy as jnp
from jax import lax
from jax.experimental import pallas as pl
from jax.experimental.pallas import tpu as pltpu

F32 = jnp.float32
BF16 = jnp.bfloat16

N_EXPERTS = 32
TOP_K = 4
N_MOD = 6
SC_WIDTH = 3
CF_WIDTH = 31
POOL_WINDOWS = (2, 4, 8, 16)
SWIGLU_LIMIT = 7.0
SWIGLU_ALPHA = 1.702
NORM_EPS = 1e-6
LN_EPS = 1e-5

SUBLANES = 8
LANES = 128
MXU_DEPTH = 256
TM = 256
BM = 256
SC_HALO = 8
CF_HALO = 32
POOL_HALO = 16
ROW_CHUNK = 32
COL_CHUNK = 256
ROUTE_ROWS = 64
DUMP_BLOCKS = 3
LOOKAHEAD_BLOCKS = 2
VMEM_LIMIT = 60 * 1024 * 1024


def _rms(x, g):
    return x * lax.rsqrt(jnp.mean(x * x, axis=-1, keepdims=True) + NORM_EPS) * g


def _sigmoid(x):
    return 0.5 * jnp.tanh(0.5 * x) + 0.5


def _const_spec(shape):
    nd = len(shape)
    return pl.BlockSpec(shape, lambda *_: (0,) * nd, pipeline_mode=pl.Buffered(1))


class _CopyGroup:
    def __init__(self, copies):
        self.copies = copies

    def start(self):
        for c in self.copies:
            c.start()

    def wait(self):
        for c in self.copies:
            c.wait()


def _load_row_tiled(ref, base, rows):
    return jnp.concatenate(
        [ref[pl.ds(base + s, rows, stride=SUBLANES), :] for s in range(SUBLANES)], axis=-1)


def _store_row_tiled(ref, base, value):
    rows = value.shape[0]
    for s in range(SUBLANES):
        ref[pl.ds(base + s, rows, stride=SUBLANES), :] = value[:, s * LANES:(s + 1) * LANES]


def _ada_kernel(c_ref, w_ref, b_ref, o_ref):
    c = c_ref[...]
    c_act = c * _sigmoid(c)
    o_ref[0, 0] = jnp.dot(c_act, w_ref[0], preferred_element_type=F32,
                          precision=lax.Precision.HIGHEST) + b_ref[0, 0]


def _ada_call(c, ada_w, ada_b):
    depth, d, _ = ada_w.shape
    bsz = c.shape[0]
    ada_b4 = ada_b.reshape(depth, N_MOD, 1, d)
    out = pl.pallas_call(
        _ada_kernel,
        grid=(depth, N_MOD),
        in_specs=[
            pl.BlockSpec((bsz, d), lambda l, j: (0, 0)),
            pl.BlockSpec((1, d, d), lambda l, j: (l, 0, j)),
            pl.BlockSpec((1, 1, 1, d), lambda l, j: (l, j, 0, 0)),
        ],
        out_specs=pl.BlockSpec((1, 1, bsz, d), lambda l, j: (l, j, 0, 0)),
        out_shape=jax.ShapeDtypeStruct((depth, N_MOD, bsz, d), F32),
        name="ada_mod",
    )(c, ada_w, ada_b4)
    return jnp.transpose(out, (0, 2, 1, 3))


def _combine(x1_ref, y4_refs, gate_ref, mod_ref, post_g):
    tm = x1_ref.shape[0]
    gate = gate_ref[...]
    y = None
    for k, y_ref in enumerate(y4_refs):
        term = gate[:, k:k + 1] * _load_row_tiled(y_ref, 0, tm)
        y = term if y is None else y + term
    return x1_ref[...] + mod_ref[0, 5:6, :] * _rms(y, post_g[...])


def _combine_specs(n, d, steps_per_seq):
    blocks_per_plane = n // TM
    specs = [pl.BlockSpec((TM, d), lambda i: (i, 0))]
    for k in range(TOP_K):
        specs.append(pl.BlockSpec((TM * SUBLANES, LANES),
                                  lambda i, k=k: (k * blocks_per_plane + i, 0)))
    specs += [
        pl.BlockSpec((TM, TOP_K), lambda i: (i, 0)),
        pl.BlockSpec((1, N_MOD, d), lambda i: (i // steps_per_seq, 0, 0)),
        pl.BlockSpec((1, d), lambda i: (0, 0)),
    ]
    return specs


def _final_kernel(x1_ref, y0, y1, y2, y3, gate_ref, mod_ref, post_g, out_ref):
    out_ref[...] = _combine(x1_ref, (y0, y1, y2, y3), gate_ref, mod_ref, post_g)


def _final_call(comb, seq_len):
    x1, y4, gate_nk, mod, post_g = comb
    n, d = x1.shape
    return pl.pallas_call(
        _final_kernel,
        grid=(n // TM,),
        in_specs=_combine_specs(n, d, seq_len // TM),
        out_specs=pl.BlockSpec((TM, d), lambda i: (i, 0)),
        out_shape=jax.ShapeDtypeStruct((n, d), F32),
        compiler_params=pltpu.CompilerParams(dimension_semantics=("arbitrary",),
                                             vmem_limit_bytes=VMEM_LIMIT),
        name="final_combine",
    )(x1, y4, y4, y4, y4, gate_nk, mod, post_g.reshape(1, d))


def _causal_conv(buf_ref, w_ref, width, halo, out_ref, shift_ref, bias_ref=None, before_chunk=()):
    tm, d = out_ref.shape
    n_rows = buf_ref.shape[0] - SUBLANES
    offsets = [halo - (width - 1) + k for k in range(width)]
    for ci, c0 in enumerate(range(0, d, COL_CHUNK)):
        if ci < len(before_chunk):
            before_chunk[ci]()
        cols = slice(c0, c0 + COL_CHUNK)
        for q in sorted({o % SUBLANES for o in offsets} - {0}):
            shift_ref[q - 1, 0:n_rows, :] = buf_ref[q:q + n_rows, cols]
        for r0 in range(0, tm, ROW_CHUNK):
            accs = [None] * (ROW_CHUNK // SUBLANES)
            for k, o in enumerate(offsets):
                q = o % SUBLANES
                w8 = w_ref[k, :, cols]
                for j in range(len(accs)):
                    start = o - q + r0 + j * SUBLANES
                    if q == 0:
                        rows = buf_ref[start:start + SUBLANES, cols]
                    else:
                        rows = shift_ref[q - 1, start:start + SUBLANES, :]
                    accs[j] = w8 * rows if accs[j] is None else accs[j] + w8 * rows
            for j, acc in enumerate(accs):
                if bias_ref is not None:
                    acc = acc + bias_ref[:, cols]
                out_ref[r0 + j * SUBLANES:r0 + (j + 1) * SUBLANES, cols] = acc


def _mix_kernel(*refs, steps_per_seq, has_comb):
    n_in = 8 if has_comb else 1
    x_in = refs[:n_in]
    (mod_ref, pre_g, post_g, w_in, sc_conv, sc_out, cf_conv, cf_conv_b, cf_ln_g, cf_ln_b, cf_out,
     cf_out_b, pool_w, pool_scale, w_o, moe_pre_g, w_router_t, b_router, band_cur, band_prev,
     x1_ref, h2_ref, eidx_ref, gate_ref, rank_ref, counts_ref,
     cx_buf, v_buf, u_buf, tmp_buf, tmp2_buf, shift_buf, carry_ref) = refs[n_in:]
    i = pl.program_id(0)
    t = i % steps_per_seq
    tm, d = x1_ref.shape

    @pl.when(t == 0)
    def _():
        cx_buf[0:SC_HALO, :] = jnp.zeros((SC_HALO, d), F32)
        v_buf[0:CF_HALO, :] = jnp.zeros((CF_HALO, d), F32)
        u_buf[...] = jnp.zeros((POOL_HALO, d), F32)

    @pl.when(i == 0)
    def _():
        carry_ref[...] = jnp.zeros(carry_ref.shape, F32)

    if has_comb:
        xp_ref, y0, y1, y2, y3, gate_prev, mod_prev, post_g_prev = x_in
        x = _combine(xp_ref, (y0, y1, y2, y3), gate_prev, mod_prev, post_g_prev)
    else:
        x = x_in[0][...]
    shift1, scale1, gate1 = mod_ref[0, 0:1, :], mod_ref[0, 1:2, :], mod_ref[0, 2:3, :]
    shift2, scale2 = mod_ref[0, 3:4, :], mod_ref[0, 4:5, :]

    hb = (_rms(x, pre_g[...]) * (1.0 + scale1) + shift1).astype(BF16)

    def proj(j):
        return jnp.dot(hb, w_in[:, j * d:(j + 1) * d], preferred_element_type=F32)

    v_buf[CF_HALO:CF_HALO + tm, :] = proj(3) * _sigmoid(proj(4))
    side = {}

    def pool_in():
        side["u"] = proj(5)

    def branch_a():
        cx_buf[SC_HALO:SC_HALO + tm, :] = proj(1) * proj(2)
        _causal_conv(cx_buf, sc_conv, SC_WIDTH, SC_HALO, tmp2_buf, shift_buf)
        cx_buf[0:SC_HALO, :] = cx_buf[tm:tm + SC_HALO, :]
        conv_a = (proj(0) * tmp2_buf[...]).astype(BF16)
        side["merged"] = _sigmoid(proj(6)) * jnp.dot(conv_a, sc_out[...], preferred_element_type=F32)

    def gate_b():
        side["gate_b"] = _sigmoid(proj(7))

    def gate_c():
        side["gate_c"] = _sigmoid(proj(8))

    _causal_conv(v_buf, cf_conv, CF_WIDTH, CF_HALO, tmp_buf, shift_buf, cf_conv_b,
                 before_chunk=(pool_in, branch_a, gate_b, gate_c))
    v_buf[0:CF_HALO, :] = v_buf[tm:tm + CF_HALO, :]
    v = tmp_buf[...]
    mu = jnp.mean(v, axis=-1, keepdims=True)
    vc = v - mu
    var = jnp.mean(vc * vc, axis=-1, keepdims=True)
    vn = vc * lax.rsqrt(var + LN_EPS) * cf_ln_g[...] + cf_ln_b[...]
    vn = vn * _sigmoid(vn)
    y_b = jnp.dot(vn.astype(BF16), cf_out[...], preferred_element_type=F32) + cf_out_b[...]
    merged = side["merged"] + side["gate_b"] * y_b

    u = side["u"]
    pos = (t * tm + 1 + lax.broadcasted_iota(jnp.int32, (tm, 1), 0)).astype(F32)
    group = d // len(POOL_WINDOWS)

    def split(a):
        hi = a.astype(BF16)
        return jnp.concatenate([hi, (a - hi.astype(F32)).astype(BF16)], axis=-1)

    y_c_parts = []
    for g, w in enumerate(POOL_WINDOWS):
        cols = slice(g * group, (g + 1) * group)
        s2 = jnp.dot(band_cur[g], split(u[:, cols]), preferred_element_type=F32)
        h2s = jnp.dot(band_prev[g], split(u_buf[:, cols]), preferred_element_type=F32)
        s = s2[:, :group] + s2[:, group:]
        s = jnp.concatenate([s[:POOL_HALO] + h2s[:, :group] + h2s[:, group:], s[POOL_HALO:]], axis=0)
        p = s * (1.0 / jnp.minimum(pos, float(w))) - u[:, cols]
        y_c_parts.append(jnp.dot(p.astype(BF16), pool_w[g], preferred_element_type=F32))
    u_buf[...] = u[tm - POOL_HALO:, :]
    y_c = jnp.concatenate(y_c_parts, axis=-1) * pool_scale[...]
    merged = merged + side["gate_c"] * y_c

    y = jnp.dot(merged.astype(BF16), w_o[...], preferred_element_type=F32)
    x1 = x + gate1 * _rms(y, post_g[...])
    x1_ref[...] = x1

    h2 = _rms(x1, moe_pre_g[...]) * (1.0 + scale2) + shift2
    _store_row_tiled(h2_ref, 0, h2)
    h2_hi = h2.astype(BF16)
    h2_split = jnp.concatenate([h2_hi, (h2 - h2_hi.astype(F32)).astype(BF16)], axis=0)
    cross = lax.dot_general(w_router_t[...], h2_split, (((1,), (1,)), ((), ())),
                            preferred_element_type=F32)
    logits = (cross[:N_EXPERTS, :tm] + cross[N_EXPERTS:, :tm] + cross[:N_EXPERTS, tm:]
              + b_router[...])
    eio = lax.broadcasted_iota(jnp.int32, (N_EXPERTS, tm), 0)
    neg_inf = jnp.float32(-jnp.inf)
    vals, idxs = [], []
    for _ in range(TOP_K):
        m = jnp.max(logits, axis=0, keepdims=True)
        idx = jnp.min(jnp.where(logits == m, eio, N_EXPERTS), axis=0, keepdims=True)
        vals.append(m)
        idxs.append(idx)
        logits = jnp.where(eio == idx, neg_inf, logits)
    exps = [jnp.exp(v - vals[0]) for v in vals]
    denom = exps[0] + exps[1] + exps[2] + exps[3]
    onehot = jnp.zeros((N_EXPERTS, tm), F32)
    for idx in idxs:
        onehot = onehot + (eio == idx).astype(F32)
    tri = (lax.broadcasted_iota(jnp.int32, (tm, tm), 0)
           < lax.broadcasted_iota(jnp.int32, (tm, tm), 1)).astype(BF16)
    carry = carry_ref[:, 0:1]
    before = jnp.dot(onehot.astype(BF16), tri, preferred_element_type=F32) + carry
    for k in range(TOP_K):
        eidx_ref[k:k + 1, :] = idxs[k]
        gate_ref[k:k + 1, :] = exps[k] / denom
        rank_ref[k:k + 1, :] = jnp.sum(jnp.where(eio == idxs[k], before, 0.0), axis=0,
                                       keepdims=True).astype(jnp.int32)
    carry = carry + jnp.sum(onehot, axis=1, keepdims=True)
    carry_ref[...] = jnp.broadcast_to(carry, carry_ref.shape)
    counts_ref[...] = jnp.broadcast_to(carry, counts_ref.shape).astype(jnp.int32)


def _split_bf16(a):
    hi = a.astype(BF16)
    return jnp.concatenate([hi, (a - hi.astype(F32)).astype(BF16)], axis=0)


def _pool_bands():
    t = jnp.arange(TM)[:, None]
    j = jnp.arange(TM)[None, :]
    th = jnp.arange(POOL_HALO)[:, None]
    jh = jnp.arange(POOL_HALO)[None, :] - POOL_HALO
    cur = jnp.stack([(j <= t) & (j > t - w) for w in POOL_WINDOWS]).astype(BF16)
    prev = jnp.stack([jh > th - w for w in POOL_WINDOWS]).astype(BF16)
    return cur, prev


def _mix_call(x2d, comb, mod, p, seq_len):
    has_comb = comb is not None
    n, d = comb[0].shape if has_comb else x2d.shape
    steps_per_seq = seq_len // TM
    n_steps = n // TM
    row = lambda a: a.reshape(1, d)
    taps = lambda a: jnp.broadcast_to(a[:, None, :], (a.shape[0], SUBLANES, d))
    if has_comb:
        x1p, y4p, gate_p, mod_p, post_g_p = comb
        lead_args = (x1p, y4p, y4p, y4p, y4p, gate_p, mod_p, row(post_g_p))
        lead_specs = _combine_specs(n, d, steps_per_seq)
    else:
        lead_args = (x2d,)
        lead_specs = [pl.BlockSpec((TM, d), lambda i: (i, 0))]
    const_args = (
        row(p["pre_g"]), row(p["post_g"]), p["w_in"], taps(p["sc_conv"]), p["sc_out"], taps(p["cf_conv"]),
        row(p["cf_conv_b"]), row(p["cf_ln_g"]), row(p["cf_ln_b"]), p["cf_out"], row(p["cf_out_b"]),
        p["pool_w"], row(p["pool_scale"]), p["w_o"], row(p["moe_pre_g"]), p["w_router_t"],
        p["b_router"].reshape(N_EXPERTS, 1), *_pool_bands(),
    )
    in_specs = (lead_specs + [pl.BlockSpec((1, N_MOD, d), lambda i: (i // steps_per_seq, 0, 0))]
                + [_const_spec(a.shape) for a in const_args])
    tok_spec = pl.BlockSpec((TOP_K, TM), lambda i: (0, i))
    out_shape = (
        jax.ShapeDtypeStruct((n, d), F32),
        jax.ShapeDtypeStruct((n * SUBLANES, LANES), F32),
        jax.ShapeDtypeStruct((TOP_K, n), jnp.int32),
        jax.ShapeDtypeStruct((TOP_K, n), F32),
        jax.ShapeDtypeStruct((TOP_K, n), jnp.int32),
        jax.ShapeDtypeStruct((N_EXPERTS, LANES), jnp.int32),
    )
    out_specs = (
        pl.BlockSpec((TM, d), lambda i: (i, 0)),
        pl.BlockSpec((TM * SUBLANES, LANES), lambda i: (i, 0)),
        tok_spec, tok_spec, tok_spec,
        pl.BlockSpec((N_EXPERTS, LANES), lambda i: (0, 0)),
    )
    return pl.pallas_call(
        functools.partial(_mix_kernel, steps_per_seq=steps_per_seq, has_comb=has_comb),
        grid=(n_steps,),
        in_specs=in_specs,
        out_specs=out_specs,
        out_shape=out_shape,
        scratch_shapes=[
            pltpu.VMEM((SC_HALO + TM, d), F32),
            pltpu.VMEM((CF_HALO + TM, d), F32),
            pltpu.VMEM((POOL_HALO, d), F32),
            pltpu.VMEM((TM, d), F32),
            pltpu.VMEM((TM, d), F32),
            pltpu.VMEM((SUBLANES - 1, CF_HALO + TM - SUBLANES, COL_CHUNK), F32),
            pltpu.VMEM((N_EXPERTS, LANES), F32),
        ],
        compiler_params=pltpu.CompilerParams(dimension_semantics=("arbitrary",),
                                             vmem_limit_bytes=VMEM_LIMIT),
        name="mixer_router",
    )(*lead_args, mod, *const_args)


def _route_kernel(pstart_ref, pvalid_ref, pend_ref, eidx_ref, rank_ref, inv_ref,
                  dest_vmem, dest_smem, sem, *, n_chunks, n_asg):
    c = pl.program_id(0)
    n_pad = inv_ref.shape[0]
    chunk = ROUTE_ROWS * LANES

    def to_smem(s):
        return pltpu.make_async_copy(dest_vmem.at[s], dest_smem.at[s], sem.at[s])

    @pl.when(c == 0)
    def _():
        def pad_slot(p, carry):
            inv_ref[p] = n_asg + (p & (2 * BM - 1))
            return carry

        def pad_expert(e, carry):
            lax.fori_loop(pvalid_ref[e], pend_ref[e], pad_slot, 0)
            return carry

        lax.fori_loop(0, N_EXPERTS, pad_expert, 0)
        lax.fori_loop(pend_ref[N_EXPERTS - 1], n_pad, pad_slot, 0)

    @pl.when(c < n_chunks)
    def _():
        eidx = eidx_ref[...]
        dest = rank_ref[...]
        for e in range(N_EXPERTS):
            dest = dest + jnp.where(eidx == e, pstart_ref[e], 0)
        dest_vmem[c % 2] = dest
        to_smem(c % 2).start()

    @pl.when(c >= 1)
    def _():
        s = (c - 1) % 2
        to_smem(s).wait()
        base = (c - 1) * chunk

        def row(r, carry):
            for col in range(LANES):
                inv_ref[dest_smem[s, r, col]] = base + r * LANES + col
            return carry

        lax.fori_loop(0, ROUTE_ROWS, row, 0)


def _routing_tables(eidx, rank, counts):
    n = eidx.shape[1]
    n_asg = n * TOP_K
    n_pad = n_asg + N_EXPERTS * BM
    nb_max = n_pad // BM
    chunk = ROUTE_ROWS * LANES
    n_chunks = n_asg // chunk
    padded = (counts + BM - 1) // BM * BM
    pend = jnp.cumsum(padded).astype(jnp.int32)
    pstart = pend - padded
    n_blocks = pend[-1:] // BM
    blk = jnp.arange(nb_max, dtype=jnp.int32)
    block_e = jnp.minimum(jnp.sum((pend[None, :] <= blk[:, None] * BM).astype(jnp.int32), axis=1),
                          N_EXPERTS - 1)
    last_e = jnp.sum(jnp.where(blk == n_blocks[0] - 1, block_e, 0))
    block_e = jnp.where(blk < n_blocks[0], block_e, last_e).astype(jnp.int32)
    used = blk < n_blocks[0]
    first = jnp.concatenate([jnp.ones((1,), bool), block_e[1:] != block_e[:-1]]) & used
    parity = (jnp.cumsum(first.astype(jnp.int32)) - 1) % 2
    later = (block_e[None, :] > block_e[:, None]) & used[None, :]
    next_e = jnp.min(jnp.where(later, block_e[None, :], N_EXPERTS), axis=1)
    next_e = jnp.where(next_e == N_EXPERTS, -1, next_e)
    block_meta = jnp.stack([block_e, first.astype(jnp.int32), parity, next_e]).astype(jnp.int32)
    smem = pl.BlockSpec(memory_space=pltpu.SMEM)
    tile = pl.BlockSpec((ROUTE_ROWS, LANES), lambda c: (jnp.minimum(c, n_chunks - 1), 0))
    inv = pl.pallas_call(
        functools.partial(_route_kernel, n_chunks=n_chunks, n_asg=n_asg),
        grid=(n_chunks + 1,),
        in_specs=[smem, smem, smem, tile, tile],
        out_specs=smem,
        out_shape=jax.ShapeDtypeStruct((n_pad + LOOKAHEAD_BLOCKS * BM,), jnp.int32),
        scratch_shapes=[
            pltpu.VMEM((2, ROUTE_ROWS, LANES), jnp.int32),
            pltpu.SMEM((2, ROUTE_ROWS, LANES), jnp.int32),
            pltpu.SemaphoreType.DMA((2,)),
        ],
        compiler_params=pltpu.CompilerParams(dimension_semantics=("arbitrary",)),
        name="route_tables",
    )(pstart, pstart + counts, pend, eidx.reshape(n_asg // LANES, LANES),
      rank.reshape(n_asg // LANES, LANES))
    return inv, block_meta, n_blocks


def _expert_kernel(meta_ref, nb_ref, inv_hbm, h2_hbm, wgu_hbm, bgu_ref, wd_hbm, bd_ref, y4_hbm,
                   idx_smem, xbuf, ybuf, wgu_f32, wd_f32, wgu_bf, wd_bf, isem, gsem, ssem, zsem, wsem,
                   *, nb_max, n_tok, layer):
    i = pl.program_id(0)
    nb = nb_ref[0]
    s_cur = i % 3
    s_next = (i + 1) % 3
    s_prev = (i + 2) % 3
    q_ahead2 = (i + 2) % 5
    q_ahead3 = (i + 3) % 5
    q_prev = (i + 4) % 5
    rows = BM * SUBLANES
    d_ff = wd_bf.shape[0]
    n_asg = TOP_K * n_tok

    def idx_copy(blk, s):
        return pltpu.make_async_copy(inv_hbm.at[pl.ds(blk * BM, BM)], idx_smem.at[s], isem.at[s])

    def gather_copy(s_idx, g, r):
        tok = idx_smem[s_idx, r] & (n_tok - 1)
        return pltpu.make_async_copy(
            h2_hbm.at[pl.ds(pl.multiple_of(tok * SUBLANES, SUBLANES), SUBLANES)],
            xbuf.at[pl.ds(pl.multiple_of(g * rows + r * SUBLANES, SUBLANES), SUBLANES)],
            gsem.at[g])

    def scatter_copy(s_idx, s, r):
        dst = idx_smem[s_idx, r]
        return pltpu.make_async_copy(
            ybuf.at[pl.ds(pl.multiple_of(s * rows + r * SUBLANES, SUBLANES), SUBLANES)],
            y4_hbm.at[pl.ds(pl.multiple_of(dst * SUBLANES, SUBLANES), SUBLANES)],
            ssem.at[s])

    def wait_gather(g):
        pltpu.make_async_copy(h2_hbm.at[pl.ds(0, rows)], xbuf.at[pl.ds(g * rows, rows)],
                              gsem.at[g]).wait()

    def wait_scatter(s):
        pltpu.make_async_copy(ybuf.at[pl.ds(s * rows, rows)], y4_hbm.at[pl.ds(0, rows)],
                              ssem.at[s]).wait()

    @pl.when(i == 0)
    def _():
        ybuf[...] = jnp.zeros(ybuf.shape, F32)
        spare = pltpu.make_async_copy(
            ybuf.at[pl.ds(0, DUMP_BLOCKS * rows)],
            y4_hbm.at[pl.ds(n_asg * SUBLANES, DUMP_BLOCKS * rows)], zsem)
        spare.start()
        spare.wait()

        def fill(r, carry):
            idx_smem[4, r] = n_asg + 2 * BM + r
            return carry
        lax.fori_loop(0, BM, fill, 0)
        for b in range(2):
            idx_copy(b, b).start()
            idx_copy(b, b).wait()

            def body(r, carry, b=b):
                gather_copy(b, b, r).start()
                return carry
            lax.fori_loop(0, BM, body, 0)
        idx_copy(2, 2).start()

    def weight_copy(e, p):
        return _CopyGroup([
            pltpu.make_async_copy(wgu_hbm.at[layer, e], wgu_f32.at[p], wsem.at[p]),
            pltpu.make_async_copy(wd_hbm.at[layer, e], wd_f32.at[p], wsem.at[p])])

    @pl.when(i == 0)
    def _():
        weight_copy(meta_ref[0, 0], 0).start()

    @pl.when((i < nb) & (meta_ref[1, i] == 1))
    def _():
        p = meta_ref[2, i]
        weight_copy(meta_ref[0, i], p).wait()
        wgu_bf[...] = wgu_f32[p].astype(BF16)
        wd_bf[...] = wd_f32[p].astype(BF16)

        @pl.when(meta_ref[3, i] >= 0)
        def _():
            weight_copy(meta_ref[3, i], 1 - p).start()

    @pl.when((i < nb) & (i >= 2))
    def _():
        wait_scatter(s_cur)

    @pl.when(i < nb)
    def _():
        idx_copy(i + 3, q_ahead3).start()
        wait_gather(s_cur)
        idx_copy(i + 2, q_ahead2).wait()
        n_chunk = d_ff // MXU_DEPTH
        halves = MXU_DEPTH // LANES
        parts = 4
        per_gather = BM // (parts * n_chunk)
        per_scatter = BM // (2 * n_chunk)
        act_base = DUMP_BLOCKS * rows
        bgu = bgu_ref[0, 0]

        def scatter_group(g):
            for r in range(g * per_scatter, (g + 1) * per_scatter):
                scatter_copy(q_prev, s_prev, r).start(priority=r % 2)

        def gather_group(g):
            for r in range(g * per_gather, (g + 1) * per_gather):
                gather_copy(q_ahead2, s_prev, r).start(priority=r % 2)

        def load_x_part(j):
            base = s_cur * rows + j * (SUBLANES // parts)
            return [xbuf[pl.ds(base + s, BM, stride=SUBLANES), :] for s in range(SUBLANES // parts)]

        for c in range(n_chunk):
            pieces = []
            for j in range(parts):
                pieces += load_x_part(j)
                gather_group(parts * c + j)
            xb = jnp.concatenate(pieces, axis=-1).astype(BF16)
            gcols = slice(c * MXU_DEPTH, (c + 1) * MXU_DEPTH)
            ucols = slice(d_ff + c * MXU_DEPTH, d_ff + (c + 1) * MXU_DEPTH)
            gt = jnp.dot(xb, wgu_bf[:, gcols], preferred_element_type=F32) + bgu[:, gcols]
            up = jnp.dot(xb, wgu_bf[:, ucols], preferred_element_type=F32) + bgu[:, ucols]
            gt = jnp.minimum(gt, SWIGLU_LIMIT)
            up = jnp.clip(up, -SWIGLU_LIMIT, SWIGLU_LIMIT)
            act_c = (up + 1.0) * gt * _sigmoid(gt * SWIGLU_ALPHA)
            for h in range(halves):
                if h == 0:
                    scatter_group(c)
                elif c == n_chunk - 1:
                    scatter_group(n_chunk)
                ybuf[pl.ds(act_base + (c * halves + h) * BM, BM), :] = act_c[:, h * LANES:(h + 1) * LANES]
        act = jnp.concatenate(
            [ybuf[pl.ds(act_base + j * BM, BM), :] for j in range(n_chunk * halves)],
            axis=-1).astype(BF16)
        for c in range(n_chunk):
            cols = slice(c * MXU_DEPTH, (c + 1) * MXU_DEPTH)
            yc = jnp.dot(act, wd_bf[:, cols], preferred_element_type=F32) + bd_ref[0, 0][:, cols]
            if c < n_chunk - 1:
                scatter_group(n_chunk + 1 + c)
            for h in range(halves):
                ybuf[pl.ds(s_cur * rows + c * halves + h, BM, stride=SUBLANES), :] = (
                    yc[:, h * LANES:(h + 1) * LANES])

    @pl.when(i == nb)
    def _():
        wait_gather(s_cur)
        wait_gather(s_next)
        idx_copy(i + 2, q_ahead2).wait()

        @pl.when(i >= 2)
        def _():
            wait_scatter(s_cur)

        def body(r, carry):
            scatter_copy(q_prev, s_prev, r).start()
            return carry
        lax.fori_loop(0, BM, body, 0)
        wait_scatter(s_next)
        wait_scatter(s_prev)


def _expert_call(layer, block_meta, n_blocks, inv, h2, w_gu, b_gu, w_down, b_down):
    n_tok = h2.shape[0] // SUBLANES
    depth, n_exp, d, two_ff = w_gu.shape
    d_ff = two_ff // 2
    nb_max = inv.shape[0] // BM - LOOKAHEAD_BLOCKS
    rows = BM * SUBLANES
    grid_spec = pltpu.PrefetchScalarGridSpec(
        num_scalar_prefetch=2,
        grid=(nb_max,),
        in_specs=[
            pl.BlockSpec(memory_space=pl.ANY),
            pl.BlockSpec(memory_space=pl.ANY),
            pl.BlockSpec(memory_space=pl.ANY),
            pl.BlockSpec((1, 1, 1, two_ff), lambda i, meta, nb: (layer, meta[0, i], 0, 0)),
            pl.BlockSpec(memory_space=pl.ANY),
            pl.BlockSpec((1, 1, 1, d), lambda i, meta, nb: (layer, meta[0, i], 0, 0)),
        ],
        out_specs=pl.BlockSpec(memory_space=pl.ANY),
        scratch_shapes=[
            pltpu.SMEM((5, BM), jnp.int32),
            pltpu.VMEM((3 * rows, LANES), F32),
            pltpu.VMEM(((DUMP_BLOCKS + 1) * rows, LANES), F32),
            pltpu.VMEM((2, d, two_ff), F32),
            pltpu.VMEM((2, d_ff, d), F32),
            pltpu.VMEM((d, two_ff), BF16),
            pltpu.VMEM((d_ff, d), BF16),
            pltpu.SemaphoreType.DMA((5,)),
            pltpu.SemaphoreType.DMA((3,)),
            pltpu.SemaphoreType.DMA((3,)),
            pltpu.SemaphoreType.DMA(()),
            pltpu.SemaphoreType.DMA((2,)),
        ],
    )
    return pl.pallas_call(
        functools.partial(_expert_kernel, nb_max=nb_max, n_tok=n_tok, layer=layer),
        grid_spec=grid_spec,
        out_shape=jax.ShapeDtypeStruct(((TOP_K * n_tok + DUMP_BLOCKS * BM) * SUBLANES, LANES), F32),
        compiler_params=pltpu.CompilerParams(dimension_semantics=("arbitrary",),
                                             vmem_limit_bytes=VMEM_LIMIT),
        name="experts",
    )(block_meta, n_blocks, inv, h2, w_gu, b_gu.reshape(depth, n_exp, 1, two_ff), w_down,
      b_down.reshape(depth, n_exp, 1, d))


def kernel(x, c, ada_w, ada_b, mix_pre_g, mix_post_g, w_in, sc_conv, sc_out, cf_conv, cf_conv_b, cf_ln_g, cf_ln_b, cf_out, cf_out_b, pool_w, pool_scale, w_o, moe_pre_g, moe_post_g, w_router, b_router, w_gu, b_gu, w_down, b_down):
    bsz, seq_len, d = x.shape
    depth = ada_w.shape[0]
    n_tok = bsz * seq_len
    assert d == SUBLANES * LANES and seq_len % TM == 0 and d % COL_CHUNK == 0 and TM % ROW_CHUNK == 0
    assert (n_tok * TOP_K) % (ROUTE_ROWS * LANES) == 0 and (n_tok * TOP_K) % BM == 0
    assert n_tok >= 2 * BM and n_tok & (n_tok - 1) == 0 and BM & (BM - 1) == 0
    mods = _ada_call(c, ada_w, ada_b)
    xs = x.reshape(n_tok, d)
    comb = None
    for l in range(depth):
        p = dict(
            pre_g=mix_pre_g[l], post_g=mix_post_g[l], w_in=w_in[l].astype(BF16), sc_conv=sc_conv[l],
            sc_out=sc_out[l].astype(BF16), cf_conv=cf_conv[l], cf_conv_b=cf_conv_b[l], cf_ln_g=cf_ln_g[l],
            cf_ln_b=cf_ln_b[l], cf_out=cf_out[l].astype(BF16), cf_out_b=cf_out_b[l],
            pool_w=pool_w[l].astype(BF16), pool_scale=pool_scale[l], w_o=w_o[l].astype(BF16),
            moe_pre_g=moe_pre_g[l], w_router_t=_split_bf16(w_router[l].T), b_router=b_router[l],
        )
        x1, h2, eidx, gate, rank, counts = _mix_call(xs if comb is None else None, comb, mods[l], p, seq_len)
        inv, block_e, n_blocks = _routing_tables(eidx, rank, counts[:, 0])
        y4 = _expert_call(l, block_e, n_blocks, inv, h2, w_gu, b_gu, w_down, b_down)
        comb = (x1, y4, gate.T, mods[l], moe_post_g[l])
    return _final_call(comb, seq_len).reshape(bsz, seq_len, d)
```

```python
import functools

import jax
import jax.numpy as jnp
from jax import lax
from jax.experimental import pallas as pl
from jax.experimental.pallas import tpu as pltpu

F32 = jnp.float32
BF16 = jnp.bfloat16

N_EXPERTS = 32
TOP_K = 4
N_MOD = 6
SC_WIDTH = 3
CF_WIDTH = 31
POOL_WINDOWS = (2, 4, 8, 16)
SWIGLU_LIMIT = 7.0
SWIGLU_ALPHA = 1.702
NORM_EPS = 1e-6
LN_EPS = 1e-5

SUBLANES = 8
LANES = 128
MXU_DEPTH = 256
TM = 256
BM = 256
SC_HALO = 8
CF_HALO = 32
POOL_HALO = 16
ROW_CHUNK = 32
COL_CHUNK = 256
ROUTE_ROWS = 64
DUMP_BLOCKS = 3
LOOKAHEAD_BLOCKS = 2
VMEM_LIMIT = 60 * 1024 * 1024


def _rms(x, g):
    return x * lax.rsqrt(jnp.mean(x * x, axis=-1, keepdims=True) + NORM_EPS) * g


def _sigmoid(x):
    return 0.5 * jnp.tanh(0.5 * x) + 0.5


def _const_spec(shape):
    nd = len(shape)
    return pl.BlockSpec(shape, lambda *_: (0,) * nd, pipeline_mode=pl.Buffered(1))


class _CopyGroup:
    def __init__(self, copies):
        self.copies = copies

    def start(self):
        for c in self.copies:
            c.start()

    def wait(self):
        for c in self.copies:
            c.wait()


def _load_row_tiled(ref, base, rows):
    return jnp.concatenate(
        [ref[pl.ds(base + s, rows, stride=SUBLANES), :] for s in range(SUBLANES)], axis=-1)


def _store_row_tiled(ref, base, value):
    rows = value.shape[0]
    for s in range(SUBLANES):
        ref[pl.ds(base + s, rows, stride=SUBLANES), :] = value[:, s * LANES:(s + 1) * LANES]


def _ada_kernel(c_ref, w_ref, b_ref, o_ref):
    c = c_ref[...]
    c_act = c * _sigmoid(c)
    o_ref[0, 0] = jnp.dot(c_act, w_ref[0], preferred_element_type=F32,
                          precision=lax.Precision.HIGHEST) + b_ref[0, 0]


def _ada_call(c, ada_w, ada_b):
    depth, d, _ = ada_w.shape
    bsz = c.shape[0]
    ada_b4 = ada_b.reshape(depth, N_MOD, 1, d)
    out = pl.pallas_call(
        _ada_kernel,
        grid=(depth, N_MOD),
        in_specs=[
            pl.BlockSpec((bsz, d), lambda l, j: (0, 0)),
            pl.BlockSpec((1, d, d), lambda l, j: (l, 0, j)),
            pl.BlockSpec((1, 1, 1, d), lambda l, j: (l, j, 0, 0)),
        ],
        out_specs=pl.BlockSpec((1, 1, bsz, d), lambda l, j: (l, j, 0, 0)),
        out_shape=jax.ShapeDtypeStruct((depth, N_MOD, bsz, d), F32),
        name="ada_mod",
    )(c, ada_w, ada_b4)
    return jnp.transpose(out, (0, 2, 1, 3))


def _combine(x1_ref, y4_refs, gate_ref, mod_ref, post_g):
    tm = x1_ref.shape[0]
    gate = gate_ref[...]
    y = None
    for k, y_ref in enumerate(y4_refs):
        term = gate[:, k:k + 1] * _load_row_tiled(y_ref, 0, tm)
        y = term if y is None else y + term
    return x1_ref[...] + mod_ref[0, 5:6, :] * _rms(y, post_g[...])


def _combine_specs(n, d, steps_per_seq):
    blocks_per_plane = n // TM
    specs = [pl.BlockSpec((TM, d), lambda i: (i, 0))]
    for k in range(TOP_K):
        specs.append(pl.BlockSpec((TM * SUBLANES, LANES),
                                  lambda i, k=k: (k * blocks_per_plane + i, 0)))
    specs += [
        pl.BlockSpec((TM, TOP_K), lambda i: (i, 0)),
        pl.BlockSpec((1, N_MOD, d), lambda i: (i // steps_per_seq, 0, 0)),
        pl.BlockSpec((1, d), lambda i: (0, 0)),
    ]
    return specs


def _final_kernel(x1_ref, y0, y1, y2, y3, gate_ref, mod_ref, post_g, out_ref):
    out_ref[...] = _combine(x1_ref, (y0, y1, y2, y3), gate_ref, mod_ref, post_g)


def _final_call(comb, seq_len):
    x1, y4, gate_nk, mod, post_g = comb
    n, d = x1.shape
    return pl.pallas_call(
        _final_kernel,
        grid=(n // TM,),
        in_specs=_combine_specs(n, d, seq_len // TM),
        out_specs=pl.BlockSpec((TM, d), lambda i: (i, 0)),
        out_shape=jax.ShapeDtypeStruct((n, d), F32),
        compiler_params=pltpu.CompilerParams(dimension_semantics=("arbitrary",),
                                             vmem_limit_bytes=VMEM_LIMIT),
        name="final_combine",
    )(x1, y4, y4, y4, y4, gate_nk, mod, post_g.reshape(1, d))


def _causal_conv(buf_ref, w_ref, width, halo, out_ref, shift_ref, bias_ref=None, before_chunk=()):
    tm, d = out_ref.shape
    n_rows = buf_ref.shape[0] - SUBLANES
    offsets = [halo - (width - 1) + k for k in range(width)]
    for ci, c0 in enumerate(range(0, d, COL_CHUNK)):
        if ci < len(before_chunk):
            before_chunk[ci]()
        cols = slice(c0, c0 + COL_CHUNK)
        for q in sorted({o % SUBLANES for o in offsets} - {0}):
            shift_ref[q - 1, 0:n_rows, :] = buf_ref[q:q + n_rows, cols]
        for r0 in range(0, tm, ROW_CHUNK):
            accs = [None] * (ROW_CHUNK // SUBLANES)
            for k, o in enumerate(offsets):
                q = o % SUBLANES
                w8 = w_ref[k, :, cols]
                for j in range(len(accs)):
                    start = o - q + r0 + j * SUBLANES
                    if q == 0:
                        rows = buf_ref[start:start + SUBLANES, cols]
                    else:
                        rows = shift_ref[q - 1, start:start + SUBLANES, :]
                    accs[j] = w8 * rows if accs[j] is None else accs[j] + w8 * rows
            for j, acc in enumerate(accs):
                if bias_ref is not None:
                    acc = acc + bias_ref[:, cols]
                out_ref[r0 + j * SUBLANES:r0 + (j + 1) * SUBLANES, cols] = acc


def _mix_kernel(*refs, steps_per_seq, has_comb):
    n_in = 8 if has_comb else 1
    x_in = refs[:n_in]
    (mod_ref, pre_g, post_g, w_in, sc_conv, sc_out, cf_conv, cf_conv_b, cf_ln_g, cf_ln_b, cf_out,
     cf_out_b, pool_w, pool_scale, w_o, moe_pre_g, w_router_t, b_router, band_cur, band_prev,
     x1_ref, h2_ref, eidx_ref, gate_ref, rank_ref, counts_ref,
     cx_buf, v_buf, u_buf, tmp_buf, tmp2_buf, shift_buf, carry_ref) = refs[n_in:]
    i = pl.program_id(0)
    t = i % steps_per_seq
    tm, d = x1_ref.shape

    @pl.when(t == 0)
    def _():
        cx_buf[0:SC_HALO, :] = jnp.zeros((SC_HALO, d), F32)
        v_buf[0:CF_HALO, :] = jnp.zeros((CF_HALO, d), F32)
        u_buf[...] = jnp.zeros((POOL_HALO, d), F32)

    @pl.when(i == 0)
    def _():
        carry_ref[...] = jnp.zeros(carry_ref.shape, F32)

    if has_comb:
        xp_ref, y0, y1, y2, y3, gate_prev, mod_prev, post_g_prev = x_in
        x = _combine(xp_ref, (y0, y1, y2, y3), gate_prev, mod_prev, post_g_prev)
    else:
        x = x_in[0][...]
    shift1, scale1, gate1 = mod_ref[0, 0:1, :], mod_ref[0, 1:2, :], mod_ref[0, 2:3, :]
    shift2, scale2 = mod_ref[0, 3:4, :], mod_ref[0, 4:5, :]

    hb = (_rms(x, pre_g[...]) * (1.0 + scale1) + shift1).astype(BF16)

    def proj(j):
        return jnp.dot(hb, w_in[:, j * d:(j + 1) * d], preferred_element_type=F32)

    v_buf[CF_HALO:CF_HALO + tm, :] = proj(3) * _sigmoid(proj(4))
    side = {}

    def pool_in():
        side["u"] = proj(5)

    def branch_a():
        cx_buf[SC_HALO:SC_HALO + tm, :] = proj(1) * proj(2)
        _causal_conv(cx_buf, sc_conv, SC_WIDTH, SC_HALO, tmp2_buf, shift_buf)
        cx_buf[0:SC_HALO, :] = cx_buf[tm:tm + SC_HALO, :]
        conv_a = (proj(0) * tmp2_buf[...]).astype(BF16)
        side["merged"] = _sigmoid(proj(6)) * jnp.dot(conv_a, sc_out[...], preferred_element_type=F32)

    def gate_b():
        side["gate_b"] = _sigmoid(proj(7))

    def gate_c():
        side["gate_c"] = _sigmoid(proj(8))

    _causal_conv(v_buf, cf_conv, CF_WIDTH, CF_HALO, tmp_buf, shift_buf, cf_conv_b,
                 before_chunk=(pool_in, branch_a, gate_b, gate_c))
    v_buf[0:CF_HALO, :] = v_buf[tm:tm + CF_HALO, :]
    v = tmp_buf[...]
    mu = jnp.mean(v, axis=-1, keepdims=True)
    vc = v - mu
    var = jnp.mean(vc * vc, axis=-1, keepdims=True)
    vn = vc * lax.rsqrt(var + LN_EPS) * cf_ln_g[...] + cf_ln_b[...]
    vn = vn * _sigmoid(vn)
    y_b = jnp.dot(vn.astype(BF16), cf_out[...], preferred_element_type=F32) + cf_out_b[...]
    merged = side["merged"] + side["gate_b"] * y_b

    u = side["u"]
    pos = (t * tm + 1 + lax.broadcasted_iota(jnp.int32, (tm, 1), 0)).astype(F32)
    group = d // len(POOL_WINDOWS)

    def split(a):
        hi = a.astype(BF16)
        return jnp.concatenate([hi, (a - hi.astype(F32)).astype(BF16)], axis=-1)

    y_c_parts = []
    for g, w in enumerate(POOL_WINDOWS):
        cols = slice(g * group, (g + 1) * group)
        s2 = jnp.dot(band_cur[g], split(u[:, cols]), preferred_element_type=F32)
        h2s = jnp.dot(band_prev[g], split(u_buf[:, cols]), preferred_element_type=F32)
        s = s2[:, :group] + s2[:, group:]
        s = jnp.concatenate([s[:POOL_HALO] + h2s[:, :group] + h2s[:, group:], s[POOL_HALO:]], axis=0)
        p = s * (1.0 / jnp.minimum(pos, float(w))) - u[:, cols]
        y_c_parts.append(jnp.dot(p.astype(BF16), pool_w[g], preferred_element_type=F32))
    u_buf[...] = u[tm - POOL_HALO:, :]
    y_c = jnp.concatenate(y_c_parts, axis=-1) * pool_scale[...]
    merged = merged + side["gate_c"] * y_c

    y = jnp.dot(merged.astype(BF16), w_o[...], preferred_element_type=F32)
    x1 = x + gate1 * _rms(y, post_g[...])
    x1_ref[...] = x1

    h2 = _rms(x1, moe_pre_g[...]) * (1.0 + scale2) + shift2
    _store_row_tiled(h2_ref, 0, h2)
    h2_hi = h2.astype(BF16)
    h2_split = jnp.concatenate([h2_hi, (h2 - h2_hi.astype(F32)).astype(BF16)], axis=0)
    cross = lax.dot_general(w_router_t[...], h2_split, (((1,), (1,)), ((), ())),
                            preferred_element_type=F32)
    logits = (cross[:N_EXPERTS, :tm] + cross[N_EXPERTS:, :tm] + cross[:N_EXPERTS, tm:]
              + b_router[...])
    eio = lax.broadcasted_iota(jnp.int32, (N_EXPERTS, tm), 0)
    neg_inf = jnp.float32(-jnp.inf)
    vals, idxs = [], []
    for _ in range(TOP_K):
        m = jnp.max(logits, axis=0, keepdims=True)
        idx = jnp.min(jnp.where(logits == m, eio, N_EXPERTS), axis=0, keepdims=True)
        vals.append(m)
        idxs.append(idx)
        logits = jnp.where(eio == idx, neg_inf, logits)
    exps = [jnp.exp(v - vals[0]) for v in vals]
    denom = exps[0] + exps[1] + exps[2] + exps[3]
    onehot = jnp.zeros((N_EXPERTS, tm), F32)
    for idx in idxs:
        onehot = onehot + (eio == idx).astype(F32)
    tri = (lax.broadcasted_iota(jnp.int32, (tm, tm), 0)
           < lax.broadcasted_iota(jnp.int32, (tm, tm), 1)).astype(BF16)
    carry = carry_ref[:, 0:1]
    before = jnp.dot(onehot.astype(BF16), tri, preferred_element_type=F32) + carry
    for k in range(TOP_K):
        eidx_ref[k:k + 1, :] = idxs[k]
        gate_ref[k:k + 1, :] = exps[k] / denom
        rank_ref[k:k + 1, :] = jnp.sum(jnp.where(eio == idxs[k], before, 0.0), axis=0,
                                       keepdims=True).astype(jnp.int32)
    carry = carry + jnp.sum(onehot, axis=1, keepdims=True)
    carry_ref[...] = jnp.broadcast_to(carry, carry_ref.shape)
    counts_ref[...] = jnp.broadcast_to(carry, counts_ref.shape).astype(jnp.int32)


def _split_bf16(a):
    hi = a.astype(BF16)
    return jnp.concatenate([hi, (a - hi.astype(F32)).astype(BF16)], axis=0)


def _pool_bands():
    t = jnp.arange(TM)[:, None]
    j = jnp.arange(TM)[None, :]
    th = jnp.arange(POOL_HALO)[:, None]
    jh = jnp.arange(POOL_HALO)[None, :] - POOL_HALO
    cur = jnp.stack([(j <= t) & (j > t - w) for w in POOL_WINDOWS]).astype(BF16)
    prev = jnp.stack([jh > th - w for w in POOL_WINDOWS]).astype(BF16)
    return cur, prev


def _mix_call(x2d, comb, mod, p, seq_len):
    has_comb = comb is not None
    n, d = comb[0].shape if has_comb else x2d.shape
    steps_per_seq = seq_len // TM
    n_steps = n // TM
    row = lambda a: a.reshape(1, d)
    taps = lambda a: jnp.broadcast_to(a[:, None, :], (a.shape[0], SUBLANES, d))
    if has_comb:
        x1p, y4p, gate_p, mod_p, post_g_p = comb
        lead_args = (x1p, y4p, y4p, y4p, y4p, gate_p, mod_p, row(post_g_p))
        lead_specs = _combine_specs(n, d, steps_per_seq)
    else:
        lead_args = (x2d,)
        lead_specs = [pl.BlockSpec((TM, d), lambda i: (i, 0))]
    const_args = (
        row(p["pre_g"]), row(p["post_g"]), p["w_in"], taps(p["sc_conv"]), p["sc_out"], taps(p["cf_conv"]),
        row(p["cf_conv_b"]), row(p["cf_ln_g"]), row(p["cf_ln_b"]), p["cf_out"], row(p["cf_out_b"]),
        p["pool_w"], row(p["pool_scale"]), p["w_o"], row(p["moe_pre_g"]), p["w_router_t"],
        p["b_router"].reshape(N_EXPERTS, 1), *_pool_bands(),
    )
    in_specs = (lead_specs + [pl.BlockSpec((1, N_MOD, d), lambda i: (i // steps_per_seq, 0, 0))]
                + [_const_spec(a.shape) for a in const_args])
    tok_spec = pl.BlockSpec((TOP_K, TM), lambda i: (0, i))
    out_shape = (
        jax.ShapeDtypeStruct((n, d), F32),
        jax.ShapeDtypeStruct((n * SUBLANES, LANES), F32),
        jax.ShapeDtypeStruct((TOP_K, n), jnp.int32),
        jax.ShapeDtypeStruct((TOP_K, n), F32),
        jax.ShapeDtypeStruct((TOP_K, n), jnp.int32),
        jax.ShapeDtypeStruct((N_EXPERTS, LANES), jnp.int32),
    )
    out_specs = (
        pl.BlockSpec((TM, d), lambda i: (i, 0)),
        pl.BlockSpec((TM * SUBLANES, LANES), lambda i: (i, 0)),
        tok_spec, tok_spec, tok_spec,
        pl.BlockSpec((N_EXPERTS, LANES), lambda i: (0, 0)),
    )
    return pl.pallas_call(
        functools.partial(_mix_kernel, steps_per_seq=steps_per_seq, has_comb=has_comb),
        grid=(n_steps,),
        in_specs=in_specs,
        out_specs=out_specs,
        out_shape=out_shape,
        scratch_shapes=[
            pltpu.VMEM((SC_HALO + TM, d), F32),
            pltpu.VMEM((CF_HALO + TM, d), F32),
            pltpu.VMEM((POOL_HALO, d), F32),
            pltpu.VMEM((TM, d), F32),
            pltpu.VMEM((TM, d), F32),
            pltpu.VMEM((SUBLANES - 1, CF_HALO + TM - SUBLANES, COL_CHUNK), F32),
            pltpu.VMEM((N_EXPERTS, LANES), F32),
        ],
        compiler_params=pltpu.CompilerParams(dimension_semantics=("arbitrary",),
                                             vmem_limit_bytes=VMEM_LIMIT),
        name="mixer_router",
    )(*lead_args, mod, *const_args)


def _route_kernel(pstart_ref, pvalid_ref, pend_ref, eidx_ref, rank_ref, inv_ref,
                  dest_vmem, dest_smem, sem, *, n_chunks, n_asg):
    c = pl.program_id(0)
    n_pad = inv_ref.shape[0]
    chunk = ROUTE_ROWS * LANES

    def to_smem(s):
        return pltpu.make_async_copy(dest_vmem.at[s], dest_smem.at[s], sem.at[s])

    @pl.when(c == 0)
    def _():
        def pad_slot(p, carry):
            inv_ref[p] = n_asg + (p & (2 * BM - 1))
            return carry

        def pad_expert(e, carry):
            lax.fori_loop(pvalid_ref[e], pend_ref[e], pad_slot, 0)
            return carry

        lax.fori_loop(0, N_EXPERTS, pad_expert, 0)
        lax.fori_loop(pend_ref[N_EXPERTS - 1], n_pad, pad_slot, 0)

    @pl.when(c < n_chunks)
    def _():
        eidx = eidx_ref[...]
        dest = rank_ref[...]
        for e in range(N_EXPERTS):
            dest = dest + jnp.where(eidx == e, pstart_ref[e], 0)
        dest_vmem[c % 2] = dest
        to_smem(c % 2).start()

    @pl.when(c >= 1)
    def _():
        s = (c - 1) % 2
        to_smem(s).wait()
        base = (c - 1) * chunk

        def row(r, carry):
            for col in range(LANES):
                inv_ref[dest_smem[s, r, col]] = base + r * LANES + col
            return carry

        lax.fori_loop(0, ROUTE_ROWS, row, 0)


def _routing_tables(eidx, rank, counts):
    n = eidx.shape[1]
    n_asg = n * TOP_K
    n_pad = n_asg + N_EXPERTS * BM
    nb_max = n_pad // BM
    chunk = ROUTE_ROWS * LANES
    n_chunks = n_asg // chunk
    padded = (counts + BM - 1) // BM * BM
    pend = jnp.cumsum(padded).astype(jnp.int32)
    pstart = pend - padded
    n_blocks = pend[-1:] // BM
    blk = jnp.arange(nb_max, dtype=jnp.int32)
    block_e = jnp.minimum(jnp.sum((pend[None, :] <= blk[:, None] * BM).astype(jnp.int32), axis=1),
                          N_EXPERTS - 1)
    last_e = jnp.sum(jnp.where(blk == n_blocks[0] - 1, block_e, 0))
    block_e = jnp.where(blk < n_blocks[0], block_e, last_e).astype(jnp.int32)
    used = blk < n_blocks[0]
    first = jnp.concatenate([jnp.ones((1,), bool), block_e[1:] != block_e[:-1]]) & used
    parity = (jnp.cumsum(first.astype(jnp.int32)) - 1) % 2
    later = (block_e[None, :] > block_e[:, None]) & used[None, :]
    next_e = jnp.min(jnp.where(later, block_e[None, :], N_EXPERTS), axis=1)
    next_e = jnp.where(next_e == N_EXPERTS, -1, next_e)
    block_meta = jnp.stack([block_e, first.astype(jnp.int32), parity, next_e]).astype(jnp.int32)
    smem = pl.BlockSpec(memory_space=pltpu.SMEM)
    tile = pl.BlockSpec((ROUTE_ROWS, LANES), lambda c: (jnp.minimum(c, n_chunks - 1), 0))
    inv = pl.pallas_call(
        functools.partial(_route_kernel, n_chunks=n_chunks, n_asg=n_asg),
        grid=(n_chunks + 1,),
        in_specs=[smem, smem, smem, tile, tile],
        out_specs=smem,
        out_shape=jax.ShapeDtypeStruct((n_pad + LOOKAHEAD_BLOCKS * BM,), jnp.int32),
        scratch_shapes=[
            pltpu.VMEM((2, ROUTE_ROWS, LANES), jnp.int32),
            pltpu.SMEM((2, ROUTE_ROWS, LANES), jnp.int32),
            pltpu.SemaphoreType.DMA((2,)),
        ],
        compiler_params=pltpu.CompilerParams(dimension_semantics=("arbitrary",)),
        name="route_tables",
    )(pstart, pstart + counts, pend, eidx.reshape(n_asg // LANES, LANES),
      rank.reshape(n_asg // LANES, LANES))
    return inv, block_meta, n_blocks


def _expert_kernel(meta_ref, nb_ref, inv_hbm, h2_hbm, wgu_hbm, bgu_ref, wd_hbm, bd_ref, y4_hbm,
                   idx_smem, xbuf, ybuf, wgu_f32, wd_f32, wgu_bf, wd_bf, isem, gsem, ssem, zsem, wsem,
                   *, nb_max, n_tok, layer):
    i = pl.program_id(0)
    nb = nb_ref[0]
    s_cur = i % 3
    s_next = (i + 1) % 3
    s_prev = (i + 2) % 3
    q_ahead2 = (i + 2) % 5
    q_ahead3 = (i + 3) % 5
    q_prev = (i + 4) % 5
    rows = BM * SUBLANES
    d_ff = wd_bf.shape[0]
    n_asg = TOP_K * n_tok

    def idx_copy(blk, s):
        return pltpu.make_async_copy(inv_hbm.at[pl.ds(blk * BM, BM)], idx_smem.at[s], isem.at[s])

    def gather_copy(s_idx, g, r):
        tok = idx_smem[s_idx, r] & (n_tok - 1)
        return pltpu.make_async_copy(
            h2_hbm.at[pl.ds(pl.multiple_of(tok * SUBLANES, SUBLANES), SUBLANES)],
            xbuf.at[pl.ds(pl.multiple_of(g * rows + r * SUBLANES, SUBLANES), SUBLANES)],
            gsem.at[g])

    def scatter_copy(s_idx, s, r):
        dst = idx_smem[s_idx, r]
        return pltpu.make_async_copy(
            ybuf.at[pl.ds(pl.multiple_of(s * rows + r * SUBLANES, SUBLANES), SUBLANES)],
            y4_hbm.at[pl.ds(pl.multiple_of(dst * SUBLANES, SUBLANES), SUBLANES)],
            ssem.at[s])

    def wait_gather(g):
        pltpu.make_async_copy(h2_hbm.at[pl.ds(0, rows)], xbuf.at[pl.ds(g * rows, rows)],
                              gsem.at[g]).wait()

    def wait_scatter(s):
        pltpu.make_async_copy(ybuf.at[pl.ds(s * rows, rows)], y4_hbm.at[pl.ds(0, rows)],
                              ssem.at[s]).wait()

    @pl.when(i == 0)
    def _():
        ybuf[...] = jnp.zeros(ybuf.shape, F32)
        spare = pltpu.make_async_copy(
            ybuf.at[pl.ds(0, DUMP_BLOCKS * rows)],
            y4_hbm.at[pl.ds(n_asg * SUBLANES, DUMP_BLOCKS * rows)], zsem)
        spare.start()
        spare.wait()

        def fill(r, carry):
            idx_smem[4, r] = n_asg + 2 * BM + r
            return carry
        lax.fori_loop(0, BM, fill, 0)
        for b in range(2):
            idx_copy(b, b).start()
            idx_copy(b, b).wait()

            def body(r, carry, b=b):
                gather_copy(b, b, r).start()
                return carry
            lax.fori_loop(0, BM, body, 0)
        idx_copy(2, 2).start()

    def weight_copy(e, p):
        return _CopyGroup([
            pltpu.make_async_copy(wgu_hbm.at[layer, e], wgu_f32.at[p], wsem.at[p]),
            pltpu.make_async_copy(wd_hbm.at[layer, e], wd_f32.at[p], wsem.at[p])])

    @pl.when(i == 0)
    def _():
        weight_copy(meta_ref[0, 0], 0).start()

    @pl.when((i < nb) & (meta_ref[1, i] == 1))
    def _():
        p = meta_ref[2, i]
        weight_copy(meta_ref[0, i], p).wait()
        wgu_bf[...] = wgu_f32[p].astype(BF16)
        wd_bf[...] = wd_f32[p].astype(BF16)

        @pl.when(meta_ref[3, i] >= 0)
        def _():
            weight_copy(meta_ref[3, i], 1 - p).start()

    @pl.when((i < nb) & (i >= 2))
    def _():
        wait_scatter(s_cur)

    @pl.when(i < nb)
    def _():
        idx_copy(i + 3, q_ahead3).start()
        wait_gather(s_cur)
        idx_copy(i + 2, q_ahead2).wait()
        n_chunk = d_ff // MXU_DEPTH
        halves = MXU_DEPTH // LANES
        parts = 4
        per_gather = BM // (parts * n_chunk // 2)
        per_scatter = BM // (2 * n_chunk)
        act_base = DUMP_BLOCKS * rows
        bgu = bgu_ref[0, 0]

        def scatter_group(g):
            for r in range(g * per_scatter, (g + 1) * per_scatter):
                scatter_copy(q_prev, s_prev, r).start(priority=r % 2)

        def gather_group(g):
            for r in range(g * per_gather, (g + 1) * per_gather):
                gather_copy(q_ahead2, s_prev, r).start(priority=r % 2)

        def load_x_part(j):
            base = s_cur * rows + j * (SUBLANES // parts)
            return [xbuf[pl.ds(base + s, BM, stride=SUBLANES), :] for s in range(SUBLANES // parts)]

        for c in range(n_chunk):
            if c % 2 == 0:
                pieces = []
                for j in range(parts):
                    pieces += load_x_part(j)
                    gather_group(parts * (c // 2) + j)
                xb = jnp.concatenate(pieces, axis=-1).astype(BF16)
            gcols = slice(c * MXU_DEPTH, (c + 1) * MXU_DEPTH)
            ucols = slice(d_ff + c * MXU_DEPTH, d_ff + (c + 1) * MXU_DEPTH)
            gt = jnp.dot(xb, wgu_bf[:, gcols], preferred_element_type=F32) + bgu[:, gcols]
            up = jnp.dot(xb, wgu_bf[:, ucols], preferred_element_type=F32) + bgu[:, ucols]
            gt = jnp.minimum(gt, SWIGLU_LIMIT)
            up = jnp.clip(up, -SWIGLU_LIMIT, SWIGLU_LIMIT)
            act_c = (up + 1.0) * gt * _sigmoid(gt * SWIGLU_ALPHA)
            for h in range(halves):
                if h == 0:
                    scatter_group(c)
                elif c == n_chunk - 1:
                    scatter_group(n_chunk)
                ybuf[pl.ds(act_base + (c * halves + h) * BM, BM), :] = act_c[:, h * LANES:(h + 1) * LANES]
        act = jnp.concatenate(
            [ybuf[pl.ds(act_base + j * BM, BM), :] for j in range(n_chunk * halves)],
            axis=-1).astype(BF16)
        for c in range(n_chunk):
            cols = slice(c * MXU_DEPTH, (c + 1) * MXU_DEPTH)
            yc = jnp.dot(act, wd_bf[:, cols], preferred_element_type=F32) + bd_ref[0, 0][:, cols]
            if c < n_chunk - 1:
                scatter_group(n_chunk + 1 + c)
            for h in range(halves):
                ybuf[pl.ds(s_cur * rows + c * halves + h, BM, stride=SUBLANES), :] = (
                    yc[:, h * LANES:(h + 1) * LANES])

    @pl.when(i == nb)
    def _():
        wait_gather(s_cur)
        wait_gather(s_next)
        idx_copy(i + 2, q_ahead2).wait()

        @pl.when(i >= 2)
        def _():
            wait_scatter(s_cur)

        def body(r, carry):
            scatter_copy(q_prev, s_prev, r).start()
            return carry
        lax.fori_loop(0, BM, body, 0)
        wait_scatter(s_next)
        wait_scatter(s_prev)


def _expert_call(layer, block_meta, n_blocks, inv, h2, w_gu, b_gu, w_down, b_down):
    n_tok = h2.shape[0] // SUBLANES
    depth, n_exp, d, two_ff = w_gu.shape
    d_ff = two_ff // 2
    nb_max = inv.shape[0] // BM - LOOKAHEAD_BLOCKS
    rows = BM * SUBLANES
    grid_spec = pltpu.PrefetchScalarGridSpec(
        num_scalar_prefetch=2,
        grid=(nb_max,),
        in_specs=[
            pl.BlockSpec(memory_space=pl.ANY),
            pl.BlockSpec(memory_space=pl.ANY),
            pl.BlockSpec(memory_space=pl.ANY),
            pl.BlockSpec((1, 1, 1, two_ff), lambda i, meta, nb: (layer, meta[0, i], 0, 0)),
            pl.BlockSpec(memory_space=pl.ANY),
            pl.BlockSpec((1, 1, 1, d), lambda i, meta, nb: (layer, meta[0, i], 0, 0)),
        ],
        out_specs=pl.BlockSpec(memory_space=pl.ANY),
        scratch_shapes=[
            pltpu.SMEM((5, BM), jnp.int32),
            pltpu.VMEM((3 * rows, LANES), F32),
            pltpu.VMEM(((DUMP_BLOCKS + 1) * rows, LANES), F32),
            pltpu.VMEM((2, d, two_ff), F32),
            pltpu.VMEM((2, d_ff, d), F32),
            pltpu.VMEM((d, two_ff), BF16),
            pltpu.VMEM((d_ff, d), BF16),
            pltpu.SemaphoreType.DMA((5,)),
            pltpu.SemaphoreType.DMA((3,)),
            pltpu.SemaphoreType.DMA((3,)),
            pltpu.SemaphoreType.DMA(()),
            pltpu.SemaphoreType.DMA((2,)),
        ],
    )
    return pl.pallas_call(
        functools.partial(_expert_kernel, nb_max=nb_max, n_tok=n_tok, layer=layer),
        grid_spec=grid_spec,
        out_shape=jax.ShapeDtypeStruct(((TOP_K * n_tok + DUMP_BLOCKS * BM) * SUBLANES, LANES), F32),
        compiler_params=pltpu.CompilerParams(dimension_semantics=("arbitrary",),
                                             vmem_limit_bytes=VMEM_LIMIT),
        name="experts",
    )(block_meta, n_blocks, inv, h2, w_gu, b_gu.reshape(depth, n_exp, 1, two_ff), w_down,
      b_down.reshape(depth, n_exp, 1, d))


def kernel(x, c, ada_w, ada_b, mix_pre_g, mix_post_g, w_in, sc_conv, sc_out, cf_conv, cf_conv_b, cf_ln_g, cf_ln_b, cf_out, cf_out_b, pool_w, pool_scale, w_o, moe_pre_g, moe_post_g, w_router, b_router, w_gu, b_gu, w_down, b_down):
    bsz, seq_len, d = x.shape
    depth = ada_w.shape[0]
    n_tok = bsz * seq_len
    assert d == SUBLANES * LANES and seq_len % TM == 0 and d % COL_CHUNK == 0 and TM % ROW_CHUNK == 0
    assert (n_tok * TOP_K) % (ROUTE_ROWS * LANES) == 0 and (n_tok * TOP_K) % BM == 0
    assert n_tok >= 2 * BM and n_tok & (n_tok - 1) == 0 and BM & (BM - 1) == 0
    mods = _ada_call(c, ada_w, ada_b)
    xs = x.reshape(n_tok, d)
    comb = None
    for l in range(depth):
        p = dict(
            pre_g=mix_pre_g[l], post_g=mix_post_g[l], w_in=w_in[l].astype(BF16), sc_conv=sc_conv[l],
            sc_out=sc_out[l].astype(BF16), cf_conv=cf_conv[l], cf_conv_b=cf_conv_b[l], cf_ln_g=cf_ln_g[l],
            cf_ln_b=cf_ln_b[l], cf_out=cf_out[l].astype(BF16), cf_out_b=cf_out_b[l],
            pool_w=pool_w[l].astype(BF16), pool_scale=pool_scale[l], w_o=w_o[l].astype(BF16),
            moe_pre_g=moe_pre_g[l], w_router_t=_split_bf16(w_router[l].T), b_router=b_router[l],
        )
        x1, h2, eidx, gate, rank, counts = _mix_call(xs if comb is None else None, comb, mods[l], p, seq_len)
        inv, block_e, n_blocks = _routing_tables(eidx, rank, counts[:, 0])
        y4 = _expert_call(l, block_e, n_blocks, inv, h2, w_gu, b_gu, w_down, b_down)
        comb = (x1, y4, gate.T, mods[l], moe_post_g[l])
    return _final_call(comb, seq_len).reshape(bsz, seq_len, d)
```
